```python
import jax, jax.numpy as jnp
from jax import lax
import numpy as np

D_MODEL = 4096
BATCH = 2
SEQ = 8192
DEPTH = 2

N_MIXERS = 2
N_POOL_LAYERS = (DEPTH + 1) // 2
N_ATTN_LAYERS = DEPTH // 2
SELF_W = 3 * D_MODEL // 4
MEM_LEN = 256
XA_HEADS = 4
XA_W = D_MODEL // 4
XA_HEAD_DIM = XA_W // XA_HEADS
POOL_WINDOWS = (2, 4, 8, 16)
N_POOL_GROUPS = len(POOL_WINDOWS)
POOL_GROUP = SELF_W // N_POOL_GROUPS
HEAD_DIM = 128
N_Q_HEADS = SELF_W // HEAD_DIM
GQA_GROUP = 8
N_KV_HEADS = N_Q_HEADS // GQA_GROUP
KV_W = N_KV_HEADS * HEAD_DIM
WINDOW = 128
BLOCK = WINDOW
ROT_DIM = HEAD_DIM // 4
ROPE_THETA = 500000.0
D_FF = 4 * D_MODEL
EPS = 1e-6
NEG = -1e30

kernel_name = "hybrid_pool_swa_memxattn_sqrelu"


def rms_norm(x, g):
    xf = x.astype(jnp.float32)
    y = xf * lax.rsqrt(jnp.mean(xf * xf, axis=-1, keepdims=True) + EPS)
    return (y * g.astype(jnp.float32)).astype(x.dtype)


def rope_tables(positions):
    inv_freq = ROPE_THETA ** (-jnp.arange(0, ROT_DIM, 2, dtype=jnp.float32) / ROT_DIM)
    ang = positions.astype(jnp.float32)[..., None] * inv_freq
    return jnp.cos(ang)[:, :, None, :], jnp.sin(ang)[:, :, None, :]


def partial_rope(x, cos, sin):
    xf = x.astype(jnp.float32)
    x1 = xf[..., : ROT_DIM // 2]
    x2 = xf[..., ROT_DIM // 2: ROT_DIM]
    rot = jnp.concatenate([x1 * cos - x2 * sin, x2 * cos + x1 * sin], axis=-1)
    return jnp.concatenate([rot, xf[..., ROT_DIM:]], axis=-1).astype(x.dtype)


def causal_pool_mixer(u, w_group, scale):
    B, S, _ = u.shape
    ug = u.reshape(B, S, N_POOL_GROUPS, POOL_GROUP)
    c0 = jnp.pad(jnp.cumsum(ug.astype(jnp.float32), axis=1), ((0, 0), (1, 0), (0, 0), (0, 0)))
    t1 = jnp.arange(1, S + 1, dtype=jnp.float32)
    means = []
    for g, w in enumerate(POOL_WINDOWS):
        cg = c0[:, :, g]
        lo = jnp.concatenate([jnp.zeros_like(cg[:, : w - 1]), cg[:, : S + 1 - w]], axis=1)
        means.append((cg[:, 1:] - lo) / jnp.minimum(t1, float(w))[None, :, None])
    pooled = jnp.stack(means, axis=2)
    p = (pooled - ug.astype(jnp.float32)).astype(u.dtype)
    y = jnp.einsum('bsgc,gcd->bsgd', p, w_group)
    return y.reshape(B, S, SELF_W) * scale


def sliding_window_gqa_sinks(q, k, v, sink):
    B, S = q.shape[:2]
    nb = S // BLOCK
    qb = q.reshape(B, nb, BLOCK, N_KV_HEADS, GQA_GROUP, HEAD_DIM)
    kb = k.reshape(B, nb, BLOCK, N_KV_HEADS, HEAD_DIM)
    vb = v.reshape(B, nb, BLOCK, N_KV_HEADS, HEAD_DIM)

    def with_prev(t):
        prev = jnp.pad(t[:, :-1], ((0, 0), (1, 0), (0, 0), (0, 0), (0, 0)))
        return jnp.concatenate([prev, t], axis=2)

    kc, vc = with_prev(kb), with_prev(vb)
    s = jnp.einsum('bnqhgd,bnkhd->bnhgqk', qb, kc).astype(jnp.float32) * (HEAD_DIM ** -0.5)
    qi = jnp.arange(BLOCK)[:, None]
    kj = jnp.arange(2 * BLOCK)[None, :]
    rel = qi + BLOCK - kj
    band = (rel >= 0) & (rel < WINDOW)
    valid = (jnp.arange(nb)[:, None, None] > 0) | (kj >= BLOCK)[None]
    mask = band[None] & valid
    s = jnp.where(mask[None, :, None, None], s, NEG)
    sink_b = jnp.broadcast_to(
        sink.astype(jnp.float32).reshape(N_KV_HEADS, GQA_GROUP)[None, None, :, :, None, None],
        s.shape[:-1] + (1,))
    p = jax.nn.softmax(jnp.concatenate([s, sink_b], axis=-1), axis=-1)[..., :-1]
    o = jnp.einsum('bnhgqk,bnkhd->bnqhgd', p.astype(v.dtype), vc)
    return o.reshape(B, S, N_Q_HEADS * HEAD_DIM)


def memory_cross_attention(xq, mem_kv):
    B, S, _ = xq.shape
    M = mem_kv.shape[1]
    q = xq.reshape(B, S, XA_HEADS, XA_HEAD_DIM)
    k = mem_kv[..., :XA_W].reshape(B, M, XA_HEADS, XA_HEAD_DIM)
    v = mem_kv[..., XA_W:].reshape(B, M, XA_HEADS, XA_HEAD_DIM)
    s = jnp.einsum('bshd,bmhd->bhsm', q, k).astype(jnp.float32) * (XA_HEAD_DIM ** -0.5)
    p = jax.nn.softmax(s, axis=-1)
    return jnp.einsum('bhsm,bmhd->bshd', p.astype(v.dtype), v).reshape(B, S, XA_W)


def setup_inputs(seed: int = 0) -> dict:
    key = jax.random.key(seed)
    ks = jax.random.split(key, 20)
    f32 = jnp.float32

    def nrm(k, shape, fan_in):
        return jax.random.normal(k, shape, f32) * (fan_in ** -0.5)

    def gain(k, shape):
        return 1.0 + 0.05 * jax.random.normal(k, shape, f32)

    x = jax.random.normal(ks[0], (BATCH, SEQ, D_MODEL), f32)
    mem = jax.random.normal(ks[1], (BATCH, MEM_LEN, D_MODEL), f32)
    offset = jax.random.randint(ks[2], (BATCH, 1), 0, 4096, dtype=jnp.int32)
    positions = offset + jnp.arange(SEQ, dtype=jnp.int32)[None, :]
    return {
        "x": x,
        "mem": mem,
        "positions": positions,
        "norm_mix": gain(ks[3], (DEPTH, D_MODEL)),
        "norm_mem": gain(ks[4], (DEPTH, D_MODEL)),
        "norm_mlp": gain(ks[5], (DEPTH, D_MODEL)),
        "w_mem_kv": nrm(ks[6], (DEPTH, D_MODEL, 2 * XA_W), D_MODEL),
        "pool_w_in": nrm(ks[7], (N_POOL_LAYERS, D_MODEL, SELF_W + XA_W), D_MODEL),
        "pool_w_group": nrm(ks[8], (N_POOL_LAYERS, N_POOL_GROUPS, POOL_GROUP, POOL_GROUP), POOL_GROUP),
        "pool_scale": 1.0 + 0.1 * jax.random.normal(ks[9], (N_POOL_LAYERS, SELF_W), f32),
        "pool_w_out": nrm(ks[10], (N_POOL_LAYERS, SELF_W + XA_W, D_MODEL), SELF_W + XA_W),
        "attn_w_in": nrm(ks[11], (N_ATTN_LAYERS, D_MODEL, SELF_W + 2 * KV_W + XA_W), D_MODEL),
        "attn_sink": 0.5 * jax.random.normal(ks[12], (N_ATTN_LAYERS, N_Q_HEADS), f32),
        "attn_w_out": nrm(ks[13], (N_ATTN_LAYERS, SELF_W + XA_W, D_MODEL), SELF_W + XA_W),
        "mlp_w1": nrm(ks[14], (DEPTH, D_MODEL, D_FF), D_MODEL),
        "mlp_w2": nrm(ks[15], (DEPTH, D_FF, D_MODEL), D_FF),
        "final_norm": gain(ks[16], (D_MODEL,)),
    }


def reference(x, mem, positions, norm_mix, norm_mem, norm_mlp, w_mem_kv,
              pool_w_in, pool_w_group, pool_scale, pool_w_out,
              attn_w_in, attn_sink, attn_w_out, mlp_w1, mlp_w2, final_norm):
    B, S, _ = x.shape
    cos, sin = rope_tables(positions)
    h = x
    for i in range(DEPTH):
        j = i // N_MIXERS
        hn = rms_norm(h, norm_mix[i])
        mem_kv = rms_norm(mem, norm_mem[i]) @ w_mem_kv[i]
        if i % N_MIXERS == 0:
            proj = hn @ pool_w_in[j]
            y_self = causal_pool_mixer(proj[..., :SELF_W], pool_w_group[j], pool_scale[j])
            xq = proj[..., SELF_W:]
            w_out = pool_w_out[j]
        else:
            proj = hn @ attn_w_in[j]
            q = proj[..., :SELF_W].reshape(B, S, N_Q_HEADS, HEAD_DIM)
            k = proj[..., SELF_W:SELF_W + KV_W].reshape(B, S, N_KV_HEADS, HEAD_DIM)
            v = proj[..., SELF_W + KV_W:SELF_W + 2 * KV_W].reshape(B, S, N_KV_HEADS, HEAD_DIM)
            xq = proj[..., SELF_W + 2 * KV_W:]
            q = partial_rope(q, cos, sin)
            k = partial_rope(k, cos, sin)
            y_self = sliding_window_gqa_sinks(q, k, v, attn_sink[j])
            w_out = attn_w_out[j]
        y_mem = memory_cross_attention(xq, mem_kv)
        h = h + jnp.concatenate([y_self, y_mem], axis=-1) @ w_out
        hn = rms_norm(h, norm_mlp[i])
        h = h + jnp.square(jax.nn.relu(hn @ mlp_w1[i])) @ mlp_w2[i]
    return rms_norm(h, final_norm)
```

```python
import functools

import jax
import jax.numpy as jnp
from jax import lax
from jax.experimental import pallas as pl
from jax.experimental.pallas import tpu as pltpu

D_MODEL = 4096
SELF_W = 3072
XA_W = 1024
XA_HEADS = 4
XA_HEAD_DIM = 256
MEM_LEN = 256
POOL_WINDOWS = (2, 4, 8, 16)
POOL_GROUP = 768
HEAD_DIM = 128
N_Q_HEADS = 24
GQA_GROUP = 8
N_KV_HEADS = 3
KV_W = 384
WINDOW = 128
ROT_DIM = 32
ROPE_THETA = 500000.0
EPS = 1e-6
NEG = -1e30

VMEM_LIMIT_BYTES = 56 * 1024 * 1024
POOL_HALO = 16


def _params(*sem):
    return pltpu.CompilerParams(dimension_semantics=sem, vmem_limit_bytes=VMEM_LIMIT_BYTES)


def _rmsnorm_kernel(x_ref, g_ref, o_ref):
    x = x_ref[...]
    ms = jnp.mean(x * x, axis=-1, keepdims=True)
    o_ref[...] = (x * lax.rsqrt(ms + EPS) * g_ref[...]).astype(o_ref.dtype)


def rmsnorm(x, g, out_dtype, rows=256):
    r, d = x.shape
    return pl.pallas_call(
        _rmsnorm_kernel,
        out_shape=jax.ShapeDtypeStruct((r, d), out_dtype),
        grid=(r // rows,),
        in_specs=[pl.BlockSpec((rows, d), lambda i: (i, 0)),
                  pl.BlockSpec((1, d), lambda i: (0, 0))],
        out_specs=pl.BlockSpec((rows, d), lambda i: (i, 0)),
        compiler_params=_params("parallel"),
        name="rmsnorm",
    )(x, g.reshape(1, d))


def _matmul_kernel(*refs, n_a, k_splits, nk, epilogue):
    a_refs = refs[:n_a]
    w_ref = refs[n_a]
    pos = n_a + 1
    res_ref = None
    if epilogue == "residual":
        res_ref = refs[pos]
        pos += 1
    o_ref = refs[pos]
    acc_ref = refs[pos + 1] if nk > 1 else None

    acc = None
    off = 0
    for a_ref, kw in zip(a_refs, k_splits):
        part = jnp.dot(a_ref[...], w_ref[off:off + kw, :], preferred_element_type=jnp.float32)
        acc = part if acc is None else acc + part
        off += kw

    def finish(total):
        if epilogue == "relu2":
            total = jnp.square(jnp.maximum(total, 0.0))
        elif epilogue == "residual":
            total = res_ref[...] + total
        o_ref[...] = total.astype(o_ref.dtype)

    if nk == 1:
        finish(acc)
    else:
        k = pl.program_id(2)

        @pl.when(k == 0)
        def _():
            acc_ref[...] = acc

        @pl.when(jnp.logical_and(k > 0, k < nk - 1))
        def _():
            acc_ref[...] += acc

        @pl.when(k == nk - 1)
        def _():
            finish(acc_ref[...] + acc)


def matmul(a_list, w, *, epilogue="cast", res=None, out_dtype=jnp.bfloat16, bm=1024, bn=1024, bk=None,
           a_resident=False):
    m = a_list[0].shape[0]
    k_total, n = w.shape
    n_a = len(a_list)
    if n_a > 1:
        assert bk is None
        k_splits = tuple(a.shape[1] for a in a_list)
        nk = 1
    else:
        bk = k_total if bk is None else bk
        k_splits = (bk,)
        nk = k_total // bk
    assert sum(k_splits) * nk == k_total and m % bm == 0 and n % bn == 0

    if nk == 1 and a_resident:
        grid = (m // bm, n // bn)
        a_specs = [pl.BlockSpec((bm, kw), lambda i, j: (i, 0)) for kw in k_splits]
        w_spec = pl.BlockSpec((k_total, bn), lambda i, j: (0, j))
        o_map = lambda i, j: (i, j)
        sem = ("parallel", "parallel")
        scratch = []
    elif nk == 1:
        grid = (n // bn, m // bm)
        a_specs = [pl.BlockSpec((bm, kw), lambda j, i: (i, 0)) for kw in k_splits]
        w_spec = pl.BlockSpec((k_total, bn), lambda j, i: (0, j))
        o_map = lambda j, i: (i, j)
        sem = ("parallel", "parallel")
        scratch = []
    else:
        grid = (n // bn, m // bm, nk)
        a_specs = [pl.BlockSpec((bm, bk), lambda j, i, k: (i, k))]
        w_spec = pl.BlockSpec((bk, bn), lambda j, i, k: (k, j))
        o_map = lambda j, i, k: (i, j)
        sem = ("parallel", "parallel", "arbitrary")
        scratch = [pltpu.VMEM((bm, bn), jnp.float32)]

    in_specs = a_specs + [w_spec]
    args = list(a_list) + [w]
    if epilogue == "residual":
        in_specs.append(pl.BlockSpec((bm, bn), o_map))
        args.append(res)
    return pl.pallas_call(
        functools.partial(_matmul_kernel, n_a=n_a, k_splits=k_splits, nk=nk, epilogue=epilogue),
        out_shape=jax.ShapeDtypeStruct((m, n), out_dtype),
        grid=grid,
        in_specs=in_specs,
        out_specs=pl.BlockSpec((bm, bn), o_map),
        scratch_shapes=scratch,
        compiler_params=_params(*sem),
        name="matmul_" + epilogue,
    )(*args)


def _pool_kernel(u_ref, halo_ref, w_ref, scale_ref, o_ref, *, ts):
    i = pl.program_id(1)
    cur = u_ref[0].astype(jnp.float32)
    halo = halo_ref[0].astype(jnp.float32)
    halo = jnp.where(i > 0, halo, 0.0)
    t1 = (i * ts + 1 + lax.broadcasted_iota(jnp.int32, (ts, 1), 0)).astype(jnp.float32)
    for g, win in enumerate(POOL_WINDOWS):
        c0, c1 = g * POOL_GROUP, (g + 1) * POOL_GROUP
        x = jnp.concatenate([halo[:, c0:c1], cur[:, c0:c1]], axis=0)
        s = x
        d = 1
        while d < win:
            s = s[d:] + s[:-d]
            d *= 2
        s = s[POOL_HALO - (win - 1):]
        mean = s / jnp.minimum(t1, float(win))
        p = (mean - cur[:, c0:c1]).astype(jnp.bfloat16)
        y = jnp.dot(p, w_ref[g], preferred_element_type=jnp.float32)
        o_ref[0, :, c0:c1] = (y * scale_ref[:, c0:c1]).astype(o_ref.dtype)


def pool_mixer(proj, w_group, scale, ts=512):
    b, s, _ = proj.shape
    hb = ts // POOL_HALO
    return pl.pallas_call(
        functools.partial(_pool_kernel, ts=ts),
        out_shape=jax.ShapeDtypeStruct((b, s, SELF_W), jnp.bfloat16),
        grid=(b, s // ts),
        in_specs=[
            pl.BlockSpec((1, ts, SELF_W), lambda bi, i: (bi, i, 0)),
            pl.BlockSpec((1, POOL_HALO, SELF_W), lambda bi, i: (bi, jnp.maximum(i * hb - 1, 0), 0)),
            pl.BlockSpec((len(POOL_WINDOWS), POOL_GROUP, POOL_GROUP), lambda bi, i: (0, 0, 0)),
            pl.BlockSpec((1, SELF_W), lambda bi, i: (0, 0)),
        ],
        out_specs=pl.BlockSpec((1, ts, SELF_W), lambda bi, i: (bi, i, 0)),
        compiler_params=_params("parallel", "parallel"),
        name="pool_mixer",
    )(proj, proj, w_group, scale.reshape(1, SELF_W))


def _xattn_kernel(q_ref, k_ref, v_ref, o_ref):
    q = q_ref[0]
    k = k_ref[0]
    v = v_ref[0]
    s = lax.dot_general(q, k, (((1,), (1,)), ((), ())), preferred_element_type=jnp.float32)
    s = s * (XA_HEAD_DIM ** -0.5)
    m = jnp.max(s, axis=-1, keepdims=True)
    e = jnp.exp(s - m)
    l = jnp.sum(e, axis=-1, keepdims=True)
    o = jnp.dot(e.astype(jnp.bfloat16), v, preferred_element_type=jnp.float32)
    o_ref[0] = (o / l).astype(o_ref.dtype)


def mem_xattn(proj, q_col0, mem_kv, ts=1024):
    b, s, _ = proj.shape
    qb = q_col0 // XA_HEAD_DIM
    assert qb * XA_HEAD_DIM == q_col0
    return pl.pallas_call(
        _xattn_kernel,
        out_shape=jax.ShapeDtypeStruct((b, s, XA_W), jnp.bfloat16),
        grid=(b, XA_HEADS, s // ts),
        in_specs=[
            pl.BlockSpec((1, ts, XA_HEAD_DIM), lambda bi, h, i: (bi, i, qb + h)),
            pl.BlockSpec((1, MEM_LEN, XA_HEAD_DIM), lambda bi, h, i: (bi, 0, h)),
            pl.BlockSpec((1, MEM_LEN, XA_HEAD_DIM), lambda bi, h, i: (bi, 0, XA_HEADS + h)),
        ],
        out_specs=pl.BlockSpec((1, ts, XA_HEAD_DIM), lambda bi, h, i: (bi, i, h)),
        compiler_params=_params("parallel", "parallel", "parallel"),
        name="mem_xattn",
    )(proj, mem_kv, mem_kv)


def _rope_table_kernel(pos_ref, freq_ref, sign_ref, cos_ref, sin_ref):
    ang = pos_ref[0].astype(jnp.float32) * freq_ref[...]
    cos_ref[0] = jnp.cos(ang)
    sin_ref[0] = jnp.sin(ang) * sign_ref[...]


def rope_tables(positions, ts=1024):
    b, s = positions.shape
    half = ROT_DIM // 2
    inv_freq = ROPE_THETA ** (-jnp.arange(0, ROT_DIM, 2, dtype=jnp.float32) / ROT_DIM)
    freq = jnp.concatenate([inv_freq, inv_freq, jnp.zeros((HEAD_DIM - ROT_DIM,), jnp.float32)]).reshape(1, HEAD_DIM)
    sign = jnp.concatenate([-jnp.ones((half,), jnp.float32), jnp.ones((half,), jnp.float32),
                            jnp.zeros((HEAD_DIM - ROT_DIM,), jnp.float32)]).reshape(1, HEAD_DIM)
    out = jax.ShapeDtypeStruct((b, s, HEAD_DIM), jnp.float32)
    return pl.pallas_call(
        _rope_table_kernel,
        out_shape=(out, out),
        grid=(b, s // ts),
        in_specs=[pl.BlockSpec((1, ts, 1), lambda bi, i: (bi, i, 0)),
                  pl.BlockSpec((1, HEAD_DIM), lambda bi, i: (0, 0)),
                  pl.BlockSpec((1, HEAD_DIM), lambda bi, i: (0, 0))],
        out_specs=(pl.BlockSpec((1, ts, HEAD_DIM), lambda bi, i: (bi, i, 0)),
                   pl.BlockSpec((1, ts, HEAD_DIM), lambda bi, i: (bi, i, 0))),
        compiler_params=_params("parallel", "parallel"),
        name="rope_tables",
    )(positions.reshape(b, s, 1), freq, sign)


def _rope(x, cos_t, sin_t, lane):
    half = ROT_DIM // 2
    partner = jnp.where(lane < half, pltpu.roll(x, HEAD_DIM - half, 1), pltpu.roll(x, half, 1))
    return x * cos_t + partner * sin_t


def _swa_kernel(sink_ref, q_ref, kc_ref, vc_ref, kp_ref, vp_ref, cosc_ref, sinc_ref, cosp_ref, sinp_ref, o_ref):
    n = pl.program_id(1)
    blk = WINDOW
    lane = lax.broadcasted_iota(jnp.int32, (blk, HEAD_DIM), 1)
    cos_c, sin_c = cosc_ref[0], sinc_ref[0]
    cos_p, sin_p = cosp_ref[0], sinp_ref[0]

    rows = GQA_GROUP * blk
    qi = lax.broadcasted_iota(jnp.int32, (rows, 2 * blk), 0) % blk
    kj = lax.broadcasted_iota(jnp.int32, (rows, 2 * blk), 1)
    rel = qi + blk - kj
    band = jnp.logical_and(rel >= 0, rel < WINDOW)
    valid = jnp.logical_or(kj >= blk, n > 0)
    mask = jnp.logical_and(band, valid)

    for h in range(N_KV_HEADS):
        kc = _rope(kc_ref[0, :, h * HEAD_DIM:(h + 1) * HEAD_DIM].astype(jnp.float32), cos_c, sin_c, lane)
        kp = _rope(kp_ref[0, :, h * HEAD_DIM:(h + 1) * HEAD_DIM].astype(jnp.float32), cos_p, sin_p, lane)
        k = jnp.concatenate([kp, kc], axis=0).astype(jnp.bfloat16)
        v = jnp.concatenate([vp_ref[0, :, h * HEAD_DIM:(h + 1) * HEAD_DIM],
                             vc_ref[0, :, h * HEAD_DIM:(h + 1) * HEAD_DIM]], axis=0)
        q_parts, sink_parts = [], []
        for g in range(GQA_GROUP):
            c0 = (h * GQA_GROUP + g) * HEAD_DIM
            q_parts.append(_rope(q_ref[0, :, c0:c0 + HEAD_DIM].astype(jnp.float32), cos_c, sin_c, lane))
            sink_parts.append(jnp.full((blk, 1), sink_ref[h * GQA_GROUP + g], jnp.float32))
        q = jnp.concatenate(q_parts, axis=0).astype(jnp.bfloat16)
        sink = jnp.concatenate(sink_parts, axis=0)
        s = lax.dot_general(q, k, (((1,), (1,)), ((), ())), preferred_element_type=jnp.float32)
        s = jnp.where(mask, s * (HEAD_DIM ** -0.5), NEG)
        m = jnp.maximum(jnp.max(s, axis=-1, keepdims=True), sink)
        e = jnp.exp(s - m)
        l = jnp.sum(e, axis=-1, keepdims=True) + jnp.exp(sink - m)
        o = jnp.dot(e.astype(jnp.bfloat16), v, preferred_element_type=jnp.float32) / l
        for g in range(GQA_GROUP):
            c0 = (h * GQA_GROUP + g) * HEAD_DIM
            o_ref[0, :, c0:c0 + HEAD_DIM] = o[g * blk:(g + 1) * blk].astype(o_ref.dtype)


def swa_attention(proj, sink, cos_t, sin_t):
    b, s, _ = proj.shape
    blk = WINDOW
    kcol = SELF_W // KV_W
    vcol = (SELF_W + KV_W) // KV_W
    assert kcol * KV_W == SELF_W
    cur = lambda bi, n, sk: (bi, n, 0)
    prev = lambda bi, n, sk: (bi, jnp.maximum(n - 1, 0), 0)
    grid_spec = pltpu.PrefetchScalarGridSpec(
        num_scalar_prefetch=1,
        grid=(b, s // blk),
        in_specs=[
            pl.BlockSpec((1, blk, SELF_W), cur),
            pl.BlockSpec((1, blk, KV_W), lambda bi, n, sk: (bi, n, kcol)),
            pl.BlockSpec((1, blk, KV_W), lambda bi, n, sk: (bi, n, vcol)),
            pl.BlockSpec((1, blk, KV_W), lambda bi, n, sk: (bi, jnp.maximum(n - 1, 0), kcol)),
            pl.BlockSpec((1, blk, KV_W), lambda bi, n, sk: (bi, jnp.maximum(n - 1, 0), vcol)),
            pl.BlockSpec((1, blk, HEAD_DIM), cur),
            pl.BlockSpec((1, blk, HEAD_DIM), cur),
            pl.BlockSpec((1, blk, HEAD_DIM), prev),
            pl.BlockSpec((1, blk, HEAD_DIM), prev),
        ],
        out_specs=pl.BlockSpec((1, blk, SELF_W), cur),
    )
    return pl.pallas_call(
        _swa_kernel,
        out_shape=jax.ShapeDtypeStruct((b, s, SELF_W), jnp.bfloat16),
        grid_spec=grid_spec,
        compiler_params=_params("parallel", "parallel"),
        name="swa_attention",
    )(sink, proj, proj, proj, proj, proj, cos_t, sin_t, cos_t, sin_t)


def kernel(x, mem, positions, norm_mix, norm_mem, norm_mlp, w_mem_kv, pool_w_in, pool_w_group, pool_scale,
           pool_w_out, attn_w_in, attn_sink, attn_w_out, mlp_w1, mlp_w2, final_norm):
    b, s, d = x.shape
    t = b * s
    bf16 = jnp.bfloat16
    h = x.reshape(t, d)
    mem2 = mem.reshape(b * MEM_LEN, d)
    cos_t, sin_t = rope_tables(positions)

    for i in range(2):
        hn = rmsnorm(h, norm_mix[i], bf16)
        memn = rmsnorm(mem2, norm_mem[i], bf16)
        mem_kv = matmul([memn], w_mem_kv[i].astype(bf16), bm=b * MEM_LEN).reshape(b, MEM_LEN, 2 * XA_W)
        if i == 0:
            proj = matmul([hn], pool_w_in[0].astype(bf16)).reshape(b, s, -1)
            y_self = pool_mixer(proj, pool_w_group[0].astype(bf16), pool_scale[0])
            y_mem = mem_xattn(proj, SELF_W, mem_kv)
            w_out = pool_w_out[0]
        else:
            proj = matmul([hn], attn_w_in[0].astype(bf16), bn=256, a_resident=True).reshape(b, s, -1)
            y_self = swa_attention(proj, attn_sink[0], cos_t, sin_t)
            y_mem = mem_xattn(proj, SELF_W + 2 * KV_W, mem_kv)
            w_out = attn_w_out[0]
        h = matmul([y_self.reshape(t, SELF_W), y_mem.reshape(t, XA_W)], w_out.astype(bf16),
                   epilogue="residual", res=h, out_dtype=jnp.float32, bn=512)
        hn = rmsnorm(h, norm_mlp[i], bf16)
        a = matmul([hn], mlp_w1[i].astype(bf16), epilogue="relu2")
        h = matmul([a], mlp_w2[i].astype(bf16), epilogue="residual", res=h, out_dtype=jnp.float32, bk=2048)
    return rmsnorm(h, final_norm, jnp.float32).reshape(b, s, d)
```

```python
import functools

import jax
import jax.numpy as jnp
from jax import lax
from jax.experimental import pallas as pl
from jax.experimental.pallas import tpu as pltpu

D_MODEL = 4096
SELF_W = 3072
XA_W = 1024
XA_HEADS = 4
XA_HEAD_DIM = 256
MEM_LEN = 256
POOL_WINDOWS = (2, 4, 8, 16)
POOL_GROUP = 768
HEAD_DIM = 128
N_Q_HEADS = 24
GQA_GROUP = 8
N_KV_HEADS = 3
KV_W = 384
WINDOW = 128
ROT_DIM = 32
ROPE_THETA = 500000.0
EPS = 1e-6
NEG = -1e30

VMEM_LIMIT_BYTES = 56 * 1024 * 1024
POOL_HALO = 16


def _params(*sem):
    return pltpu.CompilerParams(dimension_semantics=sem, vmem_limit_bytes=VMEM_LIMIT_BYTES)


def _rmsnorm_kernel(x_ref, g_ref, o_ref):
    x = x_ref[...]
    ms = jnp.mean(x * x, axis=-1, keepdims=True)
    o_ref[...] = (x * lax.rsqrt(ms + EPS) * g_ref[...]).astype(o_ref.dtype)


def rmsnorm(x, g, out_dtype, rows=256):
    r, d = x.shape
    return pl.pallas_call(
        _rmsnorm_kernel,
        out_shape=jax.ShapeDtypeStruct((r, d), out_dtype),
        grid=(r // rows,),
        in_specs=[pl.BlockSpec((rows, d), lambda i: (i, 0)),
                  pl.BlockSpec((1, d), lambda i: (0, 0))],
        out_specs=pl.BlockSpec((rows, d), lambda i: (i, 0)),
        compiler_params=_params("parallel"),
        name="rmsnorm",
    )(x, g.reshape(1, d))


def _epilogue(acc, epilogue, res_ref, o_ref):
    if epilogue == "relu2":
        acc = jnp.square(jnp.maximum(acc, 0.0))
    elif epilogue == "residual":
        acc = res_ref[...] + acc
    o_ref[...] = acc.astype(o_ref.dtype)


def _mm_wres_kernel(*refs, n_a, k_splits, chunk, n_j, epilogue):
    a_refs = refs[:n_a]
    w_ref = refs[n_a]
    res_ref = refs[n_a + 1] if epilogue == "residual" else None
    o_ref, wbf_ref = refs[-2], refs[-1]
    jp = pl.program_id(0)
    i = pl.program_id(1)
    n_slots = wbf_ref.shape[0]

    @pl.when(jp < n_j)
    def _():
        row0 = pl.multiple_of(i * chunk, chunk)
        wbf_ref[jp % n_slots, pl.ds(row0, chunk), :] = w_ref[...].astype(jnp.bfloat16)

    @pl.when(jp > 0)
    def _():
        use_slot = (jp - 1) % n_slots
        acc = None
        off = 0
        for a_ref, kw in zip(a_refs, k_splits):
            part = jnp.dot(a_ref[...], wbf_ref[use_slot, pl.ds(off, kw), :], preferred_element_type=jnp.float32)
            acc = part if acc is None else acc + part
            off += kw
        _epilogue(acc, epilogue, res_ref, o_ref)


def matmul_wres(a_list, w, layer, n_out, *, epilogue="cast", res=None, out_dtype=jnp.bfloat16, bm=1024, bn=1024):
    m = a_list[0].shape[0]
    k_total = w.shape[1]
    k_splits = tuple(a.shape[1] for a in a_list)
    n_i, n_j = m // bm, n_out // bn
    chunk = k_total // n_i
    assert sum(k_splits) == k_total and n_i * bm == m and n_j * bn == n_out and chunk * n_i == k_total
    assert chunk % 16 == 0

    def row_blk(jp, i):
        return jnp.where(jp == 0, 0, i)

    o_map = lambda jp, i: (row_blk(jp, i), jnp.maximum(jp - 1, 0))
    in_specs = [pl.BlockSpec((bm, kw), lambda jp, i: (row_blk(jp, i), 0)) for kw in k_splits]
    w_map = lambda jp, i: (layer, jnp.where(jp < n_j, i, 0), jnp.minimum(jp, n_j - 1))
    in_specs.append(pl.BlockSpec((None, chunk, bn), w_map))
    args = list(a_list) + [w]
    if epilogue == "residual":
        in_specs.append(pl.BlockSpec((bm, bn), o_map))
        args.append(res)
    return pl.pallas_call(
        functools.partial(_mm_wres_kernel, n_a=len(a_list), k_splits=k_splits, chunk=chunk, n_j=n_j,
                          epilogue=epilogue),
        out_shape=jax.ShapeDtypeStruct((m, n_out), out_dtype),
        grid=(n_j + 1, n_i),
        in_specs=in_specs,
        out_specs=pl.BlockSpec((bm, bn), o_map),
        scratch_shapes=[pltpu.VMEM((min(2, n_j), k_total, bn), jnp.bfloat16)],
        compiler_params=_params("arbitrary", "arbitrary"),
        name="mm_wres_" + epilogue,
    )(*args)


def _mm_kacc_kernel(a_ref, w_ref, res_ref, o_ref):
    @pl.when(pl.program_id(2) == 0)
    def _():
        o_ref[...] = res_ref[...]

    o_ref[...] += jnp.dot(a_ref[...], w_ref[...], preferred_element_type=jnp.float32)


def matmul_kacc(a, w, res, *, bm=1024, bn=1024, bk=2048):
    m, k_total = a.shape
    n = w.shape[1]
    assert m % bm == 0 and n % bn == 0 and k_total % bk == 0
    o_map = lambda j, i, k: (i, j)
    return pl.pallas_call(
        _mm_kacc_kernel,
        out_shape=jax.ShapeDtypeStruct((m, n), jnp.float32),
        grid=(n // bn, m // bm, k_total // bk),
        in_specs=[pl.BlockSpec((bm, bk), lambda j, i, k: (i, k)),
                  pl.BlockSpec((bk, bn), lambda j, i, k: (k, j)),
                  pl.BlockSpec((bm, bn), o_map)],
        out_specs=pl.BlockSpec((bm, bn), o_map),
        compiler_params=_params("parallel", "parallel", "arbitrary"),
        name="mm_kacc_residual",
    )(a, w, res)


def _mm_rows_kernel(a_ref, w_ref, o_ref):
    o_ref[...] = jnp.dot(a_ref[...], w_ref[...].astype(jnp.bfloat16),
                         preferred_element_type=jnp.float32).astype(o_ref.dtype)


def matmul_rows(a, w, layer, *, bn=512):
    m, k_total = a.shape
    n = w.shape[2]
    assert n % bn == 0
    return pl.pallas_call(
        _mm_rows_kernel,
        out_shape=jax.ShapeDtypeStruct((m, n), jnp.bfloat16),
        grid=(n // bn,),
        in_specs=[pl.BlockSpec((m, k_total), lambda j: (0, 0)),
                  pl.BlockSpec((None, k_total, bn), lambda j: (layer, 0, j))],
        out_specs=pl.BlockSpec((m, bn), lambda j: (0, j)),
        compiler_params=_params("parallel"),
        name="mm_rows",
    )(a, w)


def _pool_kernel(u_ref, halo_ref, w_ref, scale_ref, o_ref, *, ts):
    i = pl.program_id(1)
    cur = u_ref[0].astype(jnp.float32)
    halo = halo_ref[0].astype(jnp.float32)
    halo = jnp.where(i > 0, halo, 0.0)
    t1 = (i * ts + 1 + lax.broadcasted_iota(jnp.int32, (ts, 1), 0)).astype(jnp.float32)
    for g, win in enumerate(POOL_WINDOWS):
        c0, c1 = g * POOL_GROUP, (g + 1) * POOL_GROUP
        x = jnp.concatenate([halo[:, c0:c1], cur[:, c0:c1]], axis=0)
        s = x
        d = 1
        while d < win:
            s = s[d:] + s[:-d]
            d *= 2
        s = s[POOL_HALO - (win - 1):]
        mean = s / jnp.minimum(t1, float(win))
        p = (mean - cur[:, c0:c1]).astype(jnp.bfloat16)
        y = jnp.dot(p, w_ref[g], preferred_element_type=jnp.float32)
        o_ref[0, :, c0:c1] = (y * scale_ref[:, c0:c1]).astype(o_ref.dtype)


def pool_mixer(proj, w_group, scale, ts=512):
    b, s, _ = proj.shape
    hb = ts // POOL_HALO
    return pl.pallas_call(
        functools.partial(_pool_kernel, ts=ts),
        out_shape=jax.ShapeDtypeStruct((b, s, SELF_W), jnp.bfloat16),
        grid=(b, s // ts),
        in_specs=[
            pl.BlockSpec((1, ts, SELF_W), lambda bi, i: (bi, i, 0)),
            pl.BlockSpec((1, POOL_HALO, SELF_W), lambda bi, i: (bi, jnp.maximum(i * hb - 1, 0), 0)),
            pl.BlockSpec((len(POOL_WINDOWS), POOL_GROUP, POOL_GROUP), lambda bi, i: (0, 0, 0)),
            pl.BlockSpec((1, SELF_W), lambda bi, i: (0, 0)),
        ],
        out_specs=pl.BlockSpec((1, ts, SELF_W), lambda bi, i: (bi, i, 0)),
        compiler_params=_params("parallel", "parallel"),
        name="pool_mixer",
    )(proj, proj, w_group, scale.reshape(1, SELF_W))


def _xattn_kernel(q_ref, k_ref, v_ref, o_ref):
    q = q_ref[0]
    k = k_ref[0]
    v = v_ref[0]
    s = lax.dot_general(q, k, (((1,), (1,)), ((), ())), preferred_element_type=jnp.float32)
    s = s * (XA_HEAD_DIM ** -0.5)
    m = jnp.max(s, axis=-1, keepdims=True)
    e = jnp.exp(s - m)
    l = jnp.sum(e, axis=-1, keepdims=True)
    o = jnp.dot(e.astype(jnp.bfloat16), v, preferred_element_type=jnp.float32)
    o_ref[0] = (o / l).astype(o_ref.dtype)


def mem_xattn(proj, q_col0, mem_kv, ts=1024):
    b, s, _ = proj.shape
    qb = q_col0 // XA_HEAD_DIM
    assert qb * XA_HEAD_DIM == q_col0
    return pl.pallas_call(
        _xattn_kernel,
        out_shape=jax.ShapeDtypeStruct((b, s, XA_W), jnp.bfloat16),
        grid=(b, XA_HEADS, s // ts),
        in_specs=[
            pl.BlockSpec((1, ts, XA_HEAD_DIM), lambda bi, h, i: (bi, i, qb + h)),
            pl.BlockSpec((1, MEM_LEN, XA_HEAD_DIM), lambda bi, h, i: (bi, 0, h)),
            pl.BlockSpec((1, MEM_LEN, XA_HEAD_DIM), lambda bi, h, i: (bi, 0, XA_HEADS + h)),
        ],
        out_specs=pl.BlockSpec((1, ts, XA_HEAD_DIM), lambda bi, h, i: (bi, i, h)),
        compiler_params=_params("parallel", "parallel", "parallel"),
        name="mem_xattn",
    )(proj, mem_kv, mem_kv)


def _rope_table_kernel(pos_ref, freq_ref, sign_ref, cos_ref, sin_ref):
    ang = pos_ref[0].astype(jnp.float32) * freq_ref[...]
    cos_ref[0] = jnp.cos(ang)
    sin_ref[0] = jnp.sin(ang) * sign_ref[...]


def rope_tables(positions, ts=1024):
    b, s = positions.shape
    half = ROT_DIM // 2
    inv_freq = ROPE_THETA ** (-jnp.arange(0, ROT_DIM, 2, dtype=jnp.float32) / ROT_DIM)
    freq = jnp.concatenate([inv_freq, inv_freq, jnp.zeros((HEAD_DIM - ROT_DIM,), jnp.float32)]).reshape(1, HEAD_DIM)
    sign = jnp.concatenate([-jnp.ones((half,), jnp.float32), jnp.ones((half,), jnp.float32),
                            jnp.zeros((HEAD_DIM - ROT_DIM,), jnp.float32)]).reshape(1, HEAD_DIM)
    out = jax.ShapeDtypeStruct((b, s, HEAD_DIM), jnp.float32)
    return pl.pallas_call(
        _rope_table_kernel,
        out_shape=(out, out),
        grid=(b, s // ts),
        in_specs=[pl.BlockSpec((1, ts, 1), lambda bi, i: (bi, i, 0)),
                  pl.BlockSpec((1, HEAD_DIM), lambda bi, i: (0, 0)),
                  pl.BlockSpec((1, HEAD_DIM), lambda bi, i: (0, 0))],
        out_specs=(pl.BlockSpec((1, ts, HEAD_DIM), lambda bi, i: (bi, i, 0)),
                   pl.BlockSpec((1, ts, HEAD_DIM), lambda bi, i: (bi, i, 0))),
        compiler_params=_params("parallel", "parallel"),
        name="rope_tables",
    )(positions.reshape(b, s, 1), freq, sign)


def _rope(x, cos_t, sin_t, lane):
    half = ROT_DIM // 2
    partner = jnp.where(lane < half, pltpu.roll(x, HEAD_DIM - half, 1), pltpu.roll(x, half, 1))
    return x * cos_t + partner * sin_t


def _swa_kernel(sink_ref, q_ref, kc_ref, vc_ref, kp_ref, vp_ref, cosc_ref, sinc_ref, cosp_ref, sinp_ref, o_ref):
    n = pl.program_id(1)
    blk = WINDOW
    lane = lax.broadcasted_iota(jnp.int32, (blk, HEAD_DIM), 1)
    cos_c, sin_c = cosc_ref[0], sinc_ref[0]
    cos_p, sin_p = cosp_ref[0], sinp_ref[0]

    rows = GQA_GROUP * blk
    qi = lax.broadcasted_iota(jnp.int32, (rows, 2 * blk), 0) % blk
    kj = lax.broadcasted_iota(jnp.int32, (rows, 2 * blk), 1)
    rel = qi + blk - kj
    band = jnp.logical_and(rel >= 0, rel < WINDOW)
    valid = jnp.logical_or(kj >= blk, n > 0)
    mask = jnp.logical_and(band, valid)

    for h in range(N_KV_HEADS):
        kc = _rope(kc_ref[0, :, h * HEAD_DIM:(h + 1) * HEAD_DIM].astype(jnp.float32), cos_c, sin_c, lane)
        kp = _rope(kp_ref[0, :, h * HEAD_DIM:(h + 1) * HEAD_DIM].astype(jnp.float32), cos_p, sin_p, lane)
        k = jnp.concatenate([kp, kc], axis=0).astype(jnp.bfloat16)
        v = jnp.concatenate([vp_ref[0, :, h * HEAD_DIM:(h + 1) * HEAD_DIM],
                             vc_ref[0, :, h * HEAD_DIM:(h + 1) * HEAD_DIM]], axis=0)
        q_parts, sink_parts = [], []
        for g in range(GQA_GROUP):
            c0 = (h * GQA_GROUP + g) * HEAD_DIM
            q_parts.append(_rope(q_ref[0, :, c0:c0 + HEAD_DIM].astype(jnp.float32), cos_c, sin_c, lane))
            sink_parts.append(jnp.full((blk, 1), sink_ref[h * GQA_GROUP + g], jnp.float32))
        q = jnp.concatenate(q_parts, axis=0).astype(jnp.bfloat16)
        sink = jnp.concatenate(sink_parts, axis=0)
        s = lax.dot_general(q, k, (((1,), (1,)), ((), ())), preferred_element_type=jnp.float32)
        s = jnp.where(mask, s * (HEAD_DIM ** -0.5), NEG)
        m = jnp.maximum(jnp.max(s, axis=-1, keepdims=True), sink)
        e = jnp.exp(s - m)
        l = jnp.sum(e, axis=-1, keepdims=True) + jnp.exp(sink - m)
        o = jnp.dot(e.astype(jnp.bfloat16), v, preferred_element_type=jnp.float32) / l
        for g in range(GQA_GROUP):
            c0 = (h * GQA_GROUP + g) * HEAD_DIM
            o_ref[0, :, c0:c0 + HEAD_DIM] = o[g * blk:(g + 1) * blk].astype(o_ref.dtype)


def swa_attention(q, kvx, sink, cos_t, sin_t):
    b, s, _ = q.shape
    blk = WINDOW
    cur = lambda bi, n, sk: (bi, n, 0)
    prev = lambda bi, n, sk: (bi, jnp.maximum(n - 1, 0), 0)
    grid_spec = pltpu.PrefetchScalarGridSpec(
        num_scalar_prefetch=1,
        grid=(b, s // blk),
        in_specs=[
            pl.BlockSpec((1, blk, SELF_W), cur),
            pl.BlockSpec((1, blk, KV_W), cur),
            pl.BlockSpec((1, blk, KV_W), lambda bi, n, sk: (bi, n, 1)),
            pl.BlockSpec((1, blk, KV_W), prev),
            pl.BlockSpec((1, blk, KV_W), lambda bi, n, sk: (bi, jnp.maximum(n - 1, 0), 1)),
            pl.BlockSpec((1, blk, HEAD_DIM), cur),
            pl.BlockSpec((1, blk, HEAD_DIM), cur),
            pl.BlockSpec((1, blk, HEAD_DIM), prev),
            pl.BlockSpec((1, blk, HEAD_DIM), prev),
        ],
        out_specs=pl.BlockSpec((1, blk, SELF_W), cur),
    )
    return pl.pallas_call(
        _swa_kernel,
        out_shape=jax.ShapeDtypeStruct((b, s, SELF_W), jnp.bfloat16),
        grid_spec=grid_spec,
        compiler_params=_params("parallel", "parallel"),
        name="swa_attention",
    )(sink, q, kvx, kvx, kvx, kvx, cos_t, sin_t, cos_t, sin_t)


def kernel(x, mem, positions, norm_mix, norm_mem, norm_mlp, w_mem_kv, pool_w_in, pool_w_group, pool_scale,
           pool_w_out, attn_w_in, attn_sink, attn_w_out, mlp_w1, mlp_w2, final_norm):
    b, s, d = x.shape
    t = b * s
    bf16 = jnp.bfloat16
    h = x.reshape(t, d)
    mem2 = mem.reshape(b * MEM_LEN, d)
    cos_t, sin_t = rope_tables(positions)

    for i in range(2):
        hn = rmsnorm(h, norm_mix[i], bf16)
        memn = rmsnorm(mem2, norm_mem[i], bf16)
        mem_kv = matmul_rows(memn, w_mem_kv, i).reshape(b, MEM_LEN, 2 * XA_W)
        if i == 0:
            proj = matmul_wres([hn], pool_w_in, 0, SELF_W + XA_W).reshape(b, s, -1)
            y_self = pool_mixer(proj, pool_w_group[0].astype(bf16), pool_scale[0])
            y_mem = mem_xattn(proj, SELF_W, mem_kv)
            w_out = pool_w_out
        else:
            q = matmul_wres([hn], attn_w_in, 0, SELF_W).reshape(b, s, -1)
            kvx = matmul_wres([hn], attn_w_in[:, :, SELF_W:], 0, 2 * KV_W + XA_W, bn=2 * KV_W + XA_W)
            kvx = kvx.reshape(b, s, -1)
            y_self = swa_attention(q, kvx, attn_sink[0], cos_t, sin_t)
            y_mem = mem_xattn(kvx, 2 * KV_W, mem_kv)
            w_out = attn_w_out
        h = matmul_wres([y_self.reshape(t, SELF_W), y_mem.reshape(t, XA_W)], w_out, 0, d,
                        epilogue="residual", res=h, out_dtype=jnp.float32, bm=512)
        hn = rmsnorm(h, norm_mlp[i], bf16)
        a = matmul_wres([hn], mlp_w1, i, mlp_w1.shape[2], epilogue="relu2")
        h = matmul_kacc(a, mlp_w2[i].astype(bf16), h)
    return rmsnorm(h, final_norm, jnp.float32).reshape(b, s, d)
```

```python
import functools

import jax
import jax.numpy as jnp
from jax import lax
from jax.experimental import pallas as pl
from jax.experimental.pallas import tpu as pltpu

D_MODEL = 4096
SELF_W = 3072
XA_W = 1024
XA_HEADS = 4
XA_HEAD_DIM = 256
MEM_LEN = 256
POOL_WINDOWS = (2, 4, 8, 16)
POOL_GROUP = 768
HEAD_DIM = 128
N_Q_HEADS = 24
GQA_GROUP = 8
N_KV_HEADS = 3
KV_W = 384
WINDOW = 128
ROT_DIM = 32
ROPE_THETA = 500000.0
EPS = 1e-6
NEG = -1e30

VMEM_LIMIT_BYTES = 56 * 1024 * 1024
LANES = 128
POOL_HALO = 16


def _params(*sem):
    return pltpu.CompilerParams(dimension_semantics=sem, vmem_limit_bytes=VMEM_LIMIT_BYTES)


def _rmsnorm_kernel(x_ref, g_ref, o_ref):
    x = x_ref[...]
    ms = jnp.mean(x * x, axis=-1, keepdims=True)
    o_ref[...] = (x * lax.rsqrt(ms + EPS) * g_ref[...]).astype(o_ref.dtype)


def rmsnorm(x, g, out_dtype, rows=256):
    r, d = x.shape
    return pl.pallas_call(
        _rmsnorm_kernel,
        out_shape=jax.ShapeDtypeStruct((r, d), out_dtype),
        grid=(r // rows,),
        in_specs=[pl.BlockSpec((rows, d), lambda i: (i, 0)),
                  pl.BlockSpec((1, d), lambda i: (0, 0))],
        out_specs=pl.BlockSpec((rows, d), lambda i: (i, 0)),
        compiler_params=_params("parallel"),
        name="rmsnorm",
    )(x, g.reshape(1, d))


def _epilogue(acc, epilogue, res_ref, o_ref):
    if epilogue == "relu2":
        acc = jnp.square(jnp.maximum(acc, 0.0))
    elif epilogue == "residual":
        acc = res_ref[...] + acc
    o_ref[...] = acc.astype(o_ref.dtype)
    return acc


def _emit_norm_parts(h_new, hb_ref, pss_ref):
    hb_ref[...] = h_new.astype(hb_ref.dtype)
    sq = h_new * h_new
    pss = sq[:, :LANES]
    for c in range(1, sq.shape[1] // LANES):
        pss = pss + sq[:, c * LANES:(c + 1) * LANES]
    pss_ref[...] = pss


def _row_rstd(pss_ref, d):
    return lax.rsqrt(jnp.sum(pss_ref[...], axis=-1, keepdims=True) * (1.0 / d) + EPS)


def _mm_wres_kernel(*refs, n_a, k_splits, chunk, n_j, epilogue, normed, emit_norm, side):
    refs = list(refs)
    a_refs = [refs.pop(0) for _ in range(n_a)]
    w_ref = refs.pop(0)
    g_ref = refs.pop(0) if normed else None
    pss_in_ref = refs.pop(0) if normed else None
    res_ref = refs.pop(0) if epilogue == "residual" else None
    side_in_ref = refs.pop(0) if side else None
    o_ref = refs.pop(0)
    hb_ref = refs.pop(0) if emit_norm else None
    pss_out_ref = refs.pop(0) if emit_norm else None
    side_out_ref = refs.pop(0) if side else None
    (wbf_ref,) = refs
    jp = pl.program_id(0)
    i = pl.program_id(1)
    n_slots = wbf_ref.shape[0]

    @pl.when(jp < n_j)
    def _():
        row0 = pl.multiple_of(i * chunk, chunk)
        w = w_ref[...]
        if normed:
            w = w * g_ref[...]
        wbf_ref[jp % n_slots, pl.ds(row0, chunk), :] = w.astype(jnp.bfloat16)

    @pl.when(jp > 0)
    def _():
        use_slot = (jp - 1) % n_slots
        acc = None
        off = 0
        for a_ref, kw in zip(a_refs, k_splits):
            part = jnp.dot(a_ref[...], wbf_ref[use_slot, pl.ds(off, kw), :], preferred_element_type=jnp.float32)
            acc = part if acc is None else acc + part
            off += kw
        if normed:
            acc = acc * _row_rstd(pss_in_ref, wbf_ref.shape[1])
        total = _epilogue(acc, epilogue, res_ref, o_ref)
        if emit_norm:
            _emit_norm_parts(total, hb_ref, pss_out_ref)
        if side:
            side_out_ref[...] = side_in_ref[...].astype(side_out_ref.dtype)


def matmul_wres(a_list, w, layer, n_out, *, epilogue="cast", res=None, out_dtype=jnp.bfloat16, bm=1024, bn=1024,
                norm=None, emit_norm=False, side=None):
    m = a_list[0].shape[0]
    k_total = w.shape[1]
    k_splits = tuple(a.shape[1] for a in a_list)
    n_i, n_j = m // bm, n_out // bn
    chunk = k_total // n_i
    assert sum(k_splits) == k_total and n_i * bm == m and n_j * bn == n_out and chunk * n_i == k_total
    assert chunk % 16 == 0 and (not emit_norm or epilogue == "residual")

    def row_blk(jp, i):
        return jnp.where(jp == 0, 0, i)

    def stage_blk(jp, i):
        return jnp.where(jp < n_j, i, 0)

    o_map = lambda jp, i: (row_blk(jp, i), jnp.maximum(jp - 1, 0))
    in_specs = [pl.BlockSpec((bm, kw), lambda jp, i: (row_blk(jp, i), 0)) for kw in k_splits]
    in_specs.append(pl.BlockSpec((None, chunk, bn), lambda jp, i: (layer, stage_blk(jp, i), jnp.minimum(jp, n_j - 1))))
    args = list(a_list) + [w]
    if norm is not None:
        gain, pss = norm
        in_specs.append(pl.BlockSpec((chunk, 1), lambda jp, i: (stage_blk(jp, i), 0)))
        in_specs.append(pl.BlockSpec((bm, pss.shape[1]), lambda jp, i: (row_blk(jp, i), 0)))
        args += [gain.reshape(k_total, 1), pss]
    if epilogue == "residual":
        in_specs.append(pl.BlockSpec((bm, bn), o_map))
        args.append(res)
    out_shape = [jax.ShapeDtypeStruct((m, n_out), out_dtype)]
    out_specs = [pl.BlockSpec((bm, bn), o_map)]
    if emit_norm:
        out_shape += [jax.ShapeDtypeStruct((m, n_out), jnp.bfloat16),
                      jax.ShapeDtypeStruct((m, n_j * LANES), jnp.float32)]
        out_specs += [pl.BlockSpec((bm, bn), o_map), pl.BlockSpec((bm, LANES), o_map)]
    if side is not None:
        w_side, layer2 = side
        _, k2, n2 = w_side.shape
        rows2 = k2 // (n_j * n_i)
        assert rows2 * n_j * n_i == k2 and rows2 % 16 == 0
        slab = lambda jp, i: jnp.where(jp == 0, 0, (jp - 1) * n_i + i)
        in_specs.append(pl.BlockSpec((None, rows2, n2), lambda jp, i: (layer2, slab(jp, i), 0)))
        args.append(w_side)
        out_shape.append(jax.ShapeDtypeStruct((k2, n2), jnp.bfloat16))
        out_specs.append(pl.BlockSpec((rows2, n2), lambda jp, i: (slab(jp, i), 0)))
    outs = pl.pallas_call(
        functools.partial(_mm_wres_kernel, n_a=len(a_list), k_splits=k_splits, chunk=chunk, n_j=n_j,
                          epilogue=epilogue, normed=norm is not None, emit_norm=emit_norm, side=side is not None),
        out_shape=out_shape,
        grid=(n_j + 1, n_i),
        in_specs=in_specs,
        out_specs=out_specs,
        scratch_shapes=[pltpu.VMEM((min(2, n_j), k_total, bn), jnp.bfloat16)],
        compiler_params=_params("arbitrary", "arbitrary"),
        name="mm_wres_" + epilogue,
    )(*args)
    return outs[0] if len(outs) == 1 else tuple(outs)


def _mm_kacc_kernel(a_ref, w_ref, res_ref, o_ref, *norm_refs, nk):
    k = pl.program_id(2)

    @pl.when(k == 0)
    def _():
        o_ref[...] = res_ref[...]

    o_ref[...] += jnp.dot(a_ref[...], w_ref[...], preferred_element_type=jnp.float32)

    if norm_refs:
        @pl.when(k == nk - 1)
        def _():
            _emit_norm_parts(o_ref[...], *norm_refs)


def matmul_kacc(a, w, res, *, emit_norm=False, bm=1024, bn=1024, bk=2048):
    m, k_total = a.shape
    n = w.shape[1]
    assert m % bm == 0 and n % bn == 0 and k_total % bk == 0
    o_map = lambda j, i, k: (i, j)
    out_shape = [jax.ShapeDtypeStruct((m, n), jnp.float32)]
    out_specs = [pl.BlockSpec((bm, bn), o_map)]
    if emit_norm:
        out_shape += [jax.ShapeDtypeStruct((m, n), jnp.bfloat16),
                      jax.ShapeDtypeStruct((m, (n // bn) * LANES), jnp.float32)]
        out_specs += [pl.BlockSpec((bm, bn), o_map), pl.BlockSpec((bm, LANES), o_map)]
    outs = pl.pallas_call(
        functools.partial(_mm_kacc_kernel, nk=k_total // bk),
        out_shape=out_shape,
        grid=(n // bn, m // bm, k_total // bk),
        in_specs=[pl.BlockSpec((bm, bk), lambda j, i, k: (i, k)),
                  pl.BlockSpec((bk, bn), lambda j, i, k: (k, j)),
                  pl.BlockSpec((bm, bn), o_map)],
        out_specs=out_specs,
        compiler_params=_params("parallel", "parallel", "arbitrary"),
        name="mm_kacc_residual",
    )(a, w, res)
    return outs[0] if len(outs) == 1 else tuple(outs)


def _mm_rows_kernel(a_ref, w_ref, o_ref):
    o_ref[...] = jnp.dot(a_ref[...], w_ref[...].astype(jnp.bfloat16),
                         preferred_element_type=jnp.float32).astype(o_ref.dtype)


def matmul_rows(a, w, layer, *, bn=512):
    m, k_total = a.shape
    n = w.shape[2]
    assert n % bn == 0
    return pl.pallas_call(
        _mm_rows_kernel,
        out_shape=jax.ShapeDtypeStruct((m, n), jnp.bfloat16),
        grid=(n // bn,),
        in_specs=[pl.BlockSpec((m, k_total), lambda j: (0, 0)),
                  pl.BlockSpec((None, k_total, bn), lambda j: (layer, 0, j))],
        out_specs=pl.BlockSpec((m, bn), lambda j: (0, j)),
        compiler_params=_params("parallel"),
        name="mm_rows",
    )(a, w)


def _pool_kernel(u_ref, halo_ref, w_ref, scale_ref, o_ref, *, ts):
    i = pl.program_id(1)
    cur = u_ref[0].astype(jnp.float32)
    halo = halo_ref[0].astype(jnp.float32)
    halo = jnp.where(i > 0, halo, 0.0)
    t1 = (i * ts + 1 + lax.broadcasted_iota(jnp.int32, (ts, 1), 0)).astype(jnp.float32)
    for g, win in enumerate(POOL_WINDOWS):
        c0, c1 = g * POOL_GROUP, (g + 1) * POOL_GROUP
        x = jnp.concatenate([halo[:, c0:c1], cur[:, c0:c1]], axis=0)
        s = x
        d = 1
        while d < win:
            s = s[d:] + s[:-d]
            d *= 2
        s = s[POOL_HALO - (win - 1):]
        mean = s / jnp.minimum(t1, float(win))
        p = (mean - cur[:, c0:c1]).astype(jnp.bfloat16)
        y = jnp.dot(p, w_ref[g], preferred_element_type=jnp.float32)
        o_ref[0, :, c0:c1] = (y * scale_ref[:, c0:c1]).astype(o_ref.dtype)


def pool_mixer(proj, w_group, scale, ts=512):
    b, s, _ = proj.shape
    hb = ts // POOL_HALO
    return pl.pallas_call(
        functools.partial(_pool_kernel, ts=ts),
        out_shape=jax.ShapeDtypeStruct((b, s, SELF_W), jnp.bfloat16),
        grid=(b, s // ts),
        in_specs=[
            pl.BlockSpec((1, ts, SELF_W), lambda bi, i: (bi, i, 0)),
            pl.BlockSpec((1, POOL_HALO, SELF_W), lambda bi, i: (bi, jnp.maximum(i * hb - 1, 0), 0)),
            pl.BlockSpec((len(POOL_WINDOWS), POOL_GROUP, POOL_GROUP), lambda bi, i: (0, 0, 0)),
            pl.BlockSpec((1, SELF_W), lambda bi, i: (0, 0)),
        ],
        out_specs=pl.BlockSpec((1, ts, SELF_W), lambda bi, i: (bi, i, 0)),
        compiler_params=_params("parallel", "parallel"),
        name="pool_mixer",
    )(proj, proj, w_group, scale.reshape(1, SELF_W))


def _xattn_kernel(q_ref, k_ref, v_ref, o_ref):
    q = q_ref[0]
    k = k_ref[0]
    v = v_ref[0]
    s = lax.dot_general(q, k, (((1,), (1,)), ((), ())), preferred_element_type=jnp.float32)
    s = s * (XA_HEAD_DIM ** -0.5)
    m = jnp.max(s, axis=-1, keepdims=True)
    e = jnp.exp(s - m)
    l = jnp.sum(e, axis=-1, keepdims=True)
    o = jnp.dot(e.astype(jnp.bfloat16), v, preferred_element_type=jnp.float32)
    o_ref[0] = (o / l).astype(o_ref.dtype)


def mem_xattn(proj, q_col0, mem_kv, ts=1024):
    b, s, _ = proj.shape
    qb = q_col0 // XA_HEAD_DIM
    assert qb * XA_HEAD_DIM == q_col0
    return pl.pallas_call(
        _xattn_kernel,
        out_shape=jax.ShapeDtypeStruct((b, s, XA_W), jnp.bfloat16),
        grid=(b, XA_HEADS, s // ts),
        in_specs=[
            pl.BlockSpec((1, ts, XA_HEAD_DIM), lambda bi, h, i: (bi, i, qb + h)),
            pl.BlockSpec((1, MEM_LEN, XA_HEAD_DIM), lambda bi, h, i: (bi, 0, h)),
            pl.BlockSpec((1, MEM_LEN, XA_HEAD_DIM), lambda bi, h, i: (bi, 0, XA_HEADS + h)),
        ],
        out_specs=pl.BlockSpec((1, ts, XA_HEAD_DIM), lambda bi, h, i: (bi, i, h)),
        compiler_params=_params("parallel", "parallel", "parallel"),
        name="mem_xattn",
    )(proj, mem_kv, mem_kv)


def _rope_table_kernel(pos_ref, freq_ref, sign_ref, cos_ref, sin_ref):
    ang = pos_ref[0].astype(jnp.float32) * freq_ref[...]
    cos_ref[0] = jnp.cos(ang)
    sin_ref[0] = jnp.sin(ang) * sign_ref[...]


def rope_tables(positions, ts=1024):
    b, s = positions.shape
    half = ROT_DIM // 2
    inv_freq = ROPE_THETA ** (-jnp.arange(0, ROT_DIM, 2, dtype=jnp.float32) / ROT_DIM)
    freq = jnp.concatenate([inv_freq, inv_freq, jnp.zeros((HEAD_DIM - ROT_DIM,), jnp.float32)]).reshape(1, HEAD_DIM)
    sign = jnp.concatenate([-jnp.ones((half,), jnp.float32), jnp.ones((half,), jnp.float32),
                            jnp.zeros((HEAD_DIM - ROT_DIM,), jnp.float32)]).reshape(1, HEAD_DIM)
    out = jax.ShapeDtypeStruct((b, s, HEAD_DIM), jnp.float32)
    return pl.pallas_call(
        _rope_table_kernel,
        out_shape=(out, out),
        grid=(b, s // ts),
        in_specs=[pl.BlockSpec((1, ts, 1), lambda bi, i: (bi, i, 0)),
                  pl.BlockSpec((1, HEAD_DIM), lambda bi, i: (0, 0)),
                  pl.BlockSpec((1, HEAD_DIM), lambda bi, i: (0, 0))],
        out_specs=(pl.BlockSpec((1, ts, HEAD_DIM), lambda bi, i: (bi, i, 0)),
                   pl.BlockSpec((1, ts, HEAD_DIM), lambda bi, i: (bi, i, 0))),
        compiler_params=_params("parallel", "parallel"),
        name="rope_tables",
    )(positions.reshape(b, s, 1), freq, sign)


def _rope(x, cos_t, sin_t, lane):
    half = ROT_DIM // 2
    partner = jnp.where(lane < half, pltpu.roll(x, HEAD_DIM - half, 1), pltpu.roll(x, half, 1))
    return x * cos_t + partner * sin_t


def _swa_kernel(sink_ref, q_ref, kc_ref, vc_ref, kp_ref, vp_ref, cosc_ref, sinc_ref, cosp_ref, sinp_ref, o_ref):
    n = pl.program_id(1)
    blk = WINDOW
    lane = lax.broadcasted_iota(jnp.int32, (blk, HEAD_DIM), 1)
    cos_c, sin_c = cosc_ref[0], sinc_ref[0]
    cos_p, sin_p = cosp_ref[0], sinp_ref[0]

    rows = GQA_GROUP * blk
    qi = lax.broadcasted_iota(jnp.int32, (rows, 2 * blk), 0) % blk
    kj = lax.broadcasted_iota(jnp.int32, (rows, 2 * blk), 1)
    rel = qi + blk - kj
    band = jnp.logical_and(rel >= 0, rel < WINDOW)
    valid = jnp.logical_or(kj >= blk, n > 0)
    mask = jnp.logical_and(band, valid)

    for h in range(N_KV_HEADS):
        kc = _rope(kc_ref[0, :, h * HEAD_DIM:(h + 1) * HEAD_DIM].astype(jnp.float32), cos_c, sin_c, lane)
        kp = _rope(kp_ref[0, :, h * HEAD_DIM:(h + 1) * HEAD_DIM].astype(jnp.float32), cos_p, sin_p, lane)
        k = jnp.concatenate([kp, kc], axis=0).astype(jnp.bfloat16)
        v = jnp.concatenate([vp_ref[0, :, h * HEAD_DIM:(h + 1) * HEAD_DIM],
                             vc_ref[0, :, h * HEAD_DIM:(h + 1) * HEAD_DIM]], axis=0)
        q_parts, sink_parts = [], []
        for g in range(GQA_GROUP):
            c0 = (h * GQA_GROUP + g) * HEAD_DIM
            q_parts.append(_rope(q_ref[0, :, c0:c0 + HEAD_DIM].astype(jnp.float32), cos_c, sin_c, lane))
            sink_parts.append(jnp.full((blk, 1), sink_ref[h * GQA_GROUP + g], jnp.float32))
        q = jnp.concatenate(q_parts, axis=0).astype(jnp.bfloat16)
        sink = jnp.concatenate(sink_parts, axis=0)
        s = lax.dot_general(q, k, (((1,), (1,)), ((), ())), preferred_element_type=jnp.float32)
        s = jnp.where(mask, s * (HEAD_DIM ** -0.5), NEG)
        m = jnp.maximum(jnp.max(s, axis=-1, keepdims=True), sink)
        e = jnp.exp(s - m)
        l = jnp.sum(e, axis=-1, keepdims=True) + jnp.exp(sink - m)
        o = jnp.dot(e.astype(jnp.bfloat16), v, preferred_element_type=jnp.float32) / l
        for g in range(GQA_GROUP):
            c0 = (h * GQA_GROUP + g) * HEAD_DIM
            o_ref[0, :, c0:c0 + HEAD_DIM] = o[g * blk:(g + 1) * blk].astype(o_ref.dtype)


def swa_attention(q, kvx, sink, cos_t, sin_t):
    b, s, _ = q.shape
    blk = WINDOW
    cur = lambda bi, n, sk: (bi, n, 0)
    prev = lambda bi, n, sk: (bi, jnp.maximum(n - 1, 0), 0)
    grid_spec = pltpu.PrefetchScalarGridSpec(
        num_scalar_prefetch=1,
        grid=(b, s // blk),
        in_specs=[
            pl.BlockSpec((1, blk, SELF_W), cur),
            pl.BlockSpec((1, blk, KV_W), cur),
            pl.BlockSpec((1, blk, KV_W), lambda bi, n, sk: (bi, n, 1)),
            pl.BlockSpec((1, blk, KV_W), prev),
            pl.BlockSpec((1, blk, KV_W), lambda bi, n, sk: (bi, jnp.maximum(n - 1, 0), 1)),
            pl.BlockSpec((1, blk, HEAD_DIM), cur),
            pl.BlockSpec((1, blk, HEAD_DIM), cur),
            pl.BlockSpec((1, blk, HEAD_DIM), prev),
            pl.BlockSpec((1, blk, HEAD_DIM), prev),
        ],
        out_specs=pl.BlockSpec((1, blk, SELF_W), cur),
    )
    return pl.pallas_call(
        _swa_kernel,
        out_shape=jax.ShapeDtypeStruct((b, s, SELF_W), jnp.bfloat16),
        grid_spec=grid_spec,
        compiler_params=_params("parallel", "parallel"),
        name="swa_attention",
    )(sink, q, kvx, kvx, kvx, kvx, cos_t, sin_t, cos_t, sin_t)


def kernel(x, mem, positions, norm_mix, norm_mem, norm_mlp, w_mem_kv, pool_w_in, pool_w_group, pool_scale,
           pool_w_out, attn_w_in, attn_sink, attn_w_out, mlp_w1, mlp_w2, final_norm):
    b, s, d = x.shape
    t = b * s
    bf16 = jnp.bfloat16
    h = x.reshape(t, d)
    mem2 = mem.reshape(b * MEM_LEN, d)
    cos_t, sin_t = rope_tables(positions)

    hb = pss = None
    for i in range(2):
        memn = rmsnorm(mem2, norm_mem[i], bf16)
        mem_kv = matmul_rows(memn, w_mem_kv, i).reshape(b, MEM_LEN, 2 * XA_W)
        if i == 0:
            hn = rmsnorm(h, norm_mix[0], bf16)
            proj = matmul_wres([hn], pool_w_in, 0, SELF_W + XA_W).reshape(b, s, -1)
            y_self = pool_mixer(proj, pool_w_group[0].astype(bf16), pool_scale[0])
            y_mem = mem_xattn(proj, SELF_W, mem_kv)
            w_out = pool_w_out
        else:
            norm = (norm_mix[i], pss)
            q = matmul_wres([hb], attn_w_in, 0, SELF_W, norm=norm).reshape(b, s, -1)
            kvx = matmul_wres([hb], attn_w_in[:, :, SELF_W:], 0, 2 * KV_W + XA_W, bn=2 * KV_W + XA_W, norm=norm)
            kvx = kvx.reshape(b, s, -1)
            y_self = swa_attention(q, kvx, attn_sink[0], cos_t, sin_t)
            y_mem = mem_xattn(kvx, 2 * KV_W, mem_kv)
            w_out = attn_w_out
        h, hb, pss = matmul_wres([y_self.reshape(t, SELF_W), y_mem.reshape(t, XA_W)], w_out, 0, d,
                                 epilogue="residual", res=h, out_dtype=jnp.float32, bm=512, emit_norm=True)
        a, w2b = matmul_wres([hb], mlp_w1, i, mlp_w1.shape[2], epilogue="relu2", norm=(norm_mlp[i], pss),
                             side=(mlp_w2, i))
        if i == 0:
            h, hb, pss = matmul_kacc(a, w2b, h, emit_norm=True)
        else:
            h = matmul_kacc(a, w2b, h)
    return rmsnorm(h, final_norm, jnp.float32).reshape(b, s, d)
```

```python
import functools

import jax
import jax.numpy as jnp
from jax import lax
from jax.experimental import pallas as pl
from jax.experimental.pallas import tpu as pltpu

D_MODEL = 4096
SELF_W = 3072
XA_W = 1024
XA_HEADS = 4
XA_HEAD_DIM = 256
MEM_LEN = 256
POOL_WINDOWS = (2, 4, 8, 16)
POOL_GROUP = 768
HEAD_DIM = 128
N_Q_HEADS = 24
GQA_GROUP = 8
N_KV_HEADS = 3
KV_W = 384
WINDOW = 128
ROT_DIM = 32
ROPE_THETA = 500000.0
EPS = 1e-6
NEG = -1e30

VMEM_LIMIT_BYTES = 56 * 1024 * 1024
LANES = 128
POOL_HALO = 16


def _params(*sem):
    return pltpu.CompilerParams(dimension_semantics=sem, vmem_limit_bytes=VMEM_LIMIT_BYTES)


def _rmsnorm_kernel(x_ref, g_ref, o_ref):
    x = x_ref[...]
    ms = jnp.mean(x * x, axis=-1, keepdims=True)
    o_ref[...] = (x * lax.rsqrt(ms + EPS) * g_ref[...]).astype(o_ref.dtype)


def rmsnorm(x, g, out_dtype, rows=256):
    r, d = x.shape
    return pl.pallas_call(
        _rmsnorm_kernel,
        out_shape=jax.ShapeDtypeStruct((r, d), out_dtype),
        grid=(r // rows,),
        in_specs=[pl.BlockSpec((rows, d), lambda i: (i, 0)),
                  pl.BlockSpec((1, d), lambda i: (0, 0))],
        out_specs=pl.BlockSpec((rows, d), lambda i: (i, 0)),
        compiler_params=_params("parallel"),
        name="rmsnorm",
    )(x, g.reshape(1, d))


def _epilogue(acc, epilogue, res_ref, o_ref):
    if epilogue == "relu2":
        acc = jnp.square(jnp.maximum(acc, 0.0))
    elif epilogue == "residual":
        acc = res_ref[...] + acc
    o_ref[...] = acc.astype(o_ref.dtype)
    return acc


def _emit_norm_parts(h_new, hb_ref, pss_ref):
    hb_ref[...] = h_new.astype(hb_ref.dtype)
    sq = h_new * h_new
    pss = sq[:, :LANES]
    for c in range(1, sq.shape[1] // LANES):
        pss = pss + sq[:, c * LANES:(c + 1) * LANES]
    pss_ref[...] = pss


def _row_rstd(pss_ref, d):
    return lax.rsqrt(jnp.sum(pss_ref[...], axis=-1, keepdims=True) * (1.0 / d) + EPS)


def _mm_wres_kernel(*refs, n_a, k_splits, chunk, n_j, epilogue, normed, emit_norm, side):
    refs = list(refs)
    a_refs = [refs.pop(0) for _ in range(n_a)]
    w_ref = refs.pop(0)
    g_ref = refs.pop(0) if normed else None
    pss_in_ref = refs.pop(0) if normed else None
    res_ref = refs.pop(0) if epilogue == "residual" else None
    side_in_ref = refs.pop(0) if side else None
    o_ref = refs.pop(0)
    hb_ref = refs.pop(0) if emit_norm else None
    pss_out_ref = refs.pop(0) if emit_norm else None
    side_out_ref = refs.pop(0) if side else None
    (wbf_ref,) = refs
    jp = pl.program_id(0)
    i = pl.program_id(1)
    n_slots = wbf_ref.shape[0]

    @pl.when(jp < n_j)
    def _():
        row0 = pl.multiple_of(i * chunk, chunk)
        w = w_ref[...]
        if normed:
            w = w * g_ref[...]
        wbf_ref[jp % n_slots, pl.ds(row0, chunk), :] = w.astype(jnp.bfloat16)

    @pl.when(jp > 0)
    def _():
        use_slot = (jp - 1) % n_slots
        acc = None
        off = 0
        for a_ref, kw in zip(a_refs, k_splits):
            part = jnp.dot(a_ref[...], wbf_ref[use_slot, pl.ds(off, kw), :], preferred_element_type=jnp.float32)
            acc = part if acc is None else acc + part
            off += kw
        if normed:
            acc = acc * _row_rstd(pss_in_ref, wbf_ref.shape[1])
        total = _epilogue(acc, epilogue, res_ref, o_ref)
        if emit_norm:
            _emit_norm_parts(total, hb_ref, pss_out_ref)
        if side:
            side_out_ref[...] = side_in_ref[...].astype(side_out_ref.dtype)


def matmul_wres(a_list, w, layer, n_out, *, epilogue="cast", res=None, out_dtype=jnp.bfloat16, bm=1024, bn=1024,
                norm=None, emit_norm=False, side=None):
    m = a_list[0].shape[0]
    k_total = w.shape[1]
    k_splits = tuple(a.shape[1] for a in a_list)
    n_i, n_j = m // bm, n_out // bn
    chunk = k_total // n_i
    assert sum(k_splits) == k_total and n_i * bm == m and n_j * bn == n_out and chunk * n_i == k_total
    assert chunk % 16 == 0 and (not emit_norm or epilogue == "residual")

    def row_blk(jp, i):
        return jnp.where(jp == 0, 0, i)

    def stage_blk(jp, i):
        return jnp.where(jp < n_j, i, 0)

    o_map = lambda jp, i: (row_blk(jp, i), jnp.maximum(jp - 1, 0))
    in_specs = [pl.BlockSpec((bm, kw), lambda jp, i: (row_blk(jp, i), 0)) for kw in k_splits]
    in_specs.append(pl.BlockSpec((None, chunk, bn), lambda jp, i: (layer, stage_blk(jp, i), jnp.minimum(jp, n_j - 1))))
    args = list(a_list) + [w]
    if norm is not None:
        gain, pss = norm
        in_specs.append(pl.BlockSpec((chunk, 1), lambda jp, i: (stage_blk(jp, i), 0)))
        in_specs.append(pl.BlockSpec((bm, pss.shape[1]), lambda jp, i: (row_blk(jp, i), 0)))
        args += [gain.reshape(k_total, 1), pss]
    if epilogue == "residual":
        in_specs.append(pl.BlockSpec((bm, bn), o_map))
        args.append(res)
    out_shape = [jax.ShapeDtypeStruct((m, n_out), out_dtype)]
    out_specs = [pl.BlockSpec((bm, bn), o_map)]
    if emit_norm:
        out_shape += [jax.ShapeDtypeStruct((m, n_out), jnp.bfloat16),
                      jax.ShapeDtypeStruct((m, n_j * LANES), jnp.float32)]
        out_specs += [pl.BlockSpec((bm, bn), o_map), pl.BlockSpec((bm, LANES), o_map)]
    if side is not None:
        w_side, layer2 = side
        _, k2, n2 = w_side.shape
        rows2 = k2 // (n_j * n_i)
        assert rows2 * n_j * n_i == k2 and rows2 % 16 == 0
        slab = lambda jp, i: jnp.where(jp == 0, 0, (jp - 1) * n_i + i)
        in_specs.append(pl.BlockSpec((None, rows2, n2), lambda jp, i: (layer2, slab(jp, i), 0)))
        args.append(w_side)
        out_shape.append(jax.ShapeDtypeStruct((k2, n2), jnp.bfloat16))
        out_specs.append(pl.BlockSpec((rows2, n2), lambda jp, i: (slab(jp, i), 0)))
    outs = pl.pallas_call(
        functools.partial(_mm_wres_kernel, n_a=len(a_list), k_splits=k_splits, chunk=chunk, n_j=n_j,
                          epilogue=epilogue, normed=norm is not None, emit_norm=emit_norm, side=side is not None),
        out_shape=out_shape,
        grid=(n_j + 1, n_i),
        in_specs=in_specs,
        out_specs=out_specs,
        scratch_shapes=[pltpu.VMEM((min(2, n_j), k_total, bn), jnp.bfloat16)],
        compiler_params=_params("arbitrary", "arbitrary"),
        name="mm_wres_" + epilogue,
    )(*args)
    return outs[0] if len(outs) == 1 else tuple(outs)


def _mm_kacc_kernel(a_ref, w_ref, res_ref, o_ref, *norm_refs, nk):
    k = pl.program_id(2)

    @pl.when(k == 0)
    def _():
        o_ref[...] = res_ref[...]

    def partial_product():
        return jnp.dot(a_ref[...], w_ref[...], preferred_element_type=jnp.float32)

    if not norm_refs:
        o_ref[...] += partial_product()
    else:
        @pl.when(k < nk - 1)
        def _():
            o_ref[...] += partial_product()

        @pl.when(k == nk - 1)
        def _():
            total = o_ref[...] + partial_product()
            o_ref[...] = total
            _emit_norm_parts(total, *norm_refs)


def matmul_kacc(a, w, res, *, emit_norm=False, bm=1024, bn=1024, bk=2048):
    m, k_total = a.shape
    n = w.shape[1]
    assert m % bm == 0 and n % bn == 0 and k_total % bk == 0
    o_map = lambda j, i, k: (i, j)
    out_shape = [jax.ShapeDtypeStruct((m, n), jnp.float32)]
    out_specs = [pl.BlockSpec((bm, bn), o_map)]
    if emit_norm:
        out_shape += [jax.ShapeDtypeStruct((m, n), jnp.bfloat16),
                      jax.ShapeDtypeStruct((m, (n // bn) * LANES), jnp.float32)]
        out_specs += [pl.BlockSpec((bm, bn), o_map), pl.BlockSpec((bm, LANES), o_map)]
    outs = pl.pallas_call(
        functools.partial(_mm_kacc_kernel, nk=k_total // bk),
        out_shape=out_shape,
        grid=(n // bn, m // bm, k_total // bk),
        in_specs=[pl.BlockSpec((bm, bk), lambda j, i, k: (i, k)),
                  pl.BlockSpec((bk, bn), lambda j, i, k: (k, j)),
                  pl.BlockSpec((bm, bn), o_map)],
        out_specs=out_specs,
        compiler_params=_params("parallel", "parallel", "arbitrary"),
        name="mm_kacc_residual",
    )(a, w, res)
    return outs[0] if len(outs) == 1 else tuple(outs)


def _mm_rows_kernel(a_ref, w_ref, o_ref):
    o_ref[...] = jnp.dot(a_ref[...], w_ref[...].astype(jnp.bfloat16),
                         preferred_element_type=jnp.float32).astype(o_ref.dtype)


def matmul_rows(a, w, layer, *, bn=512):
    m, k_total = a.shape
    n = w.shape[2]
    assert n % bn == 0
    return pl.pallas_call(
        _mm_rows_kernel,
        out_shape=jax.ShapeDtypeStruct((m, n), jnp.bfloat16),
        grid=(n // bn,),
        in_specs=[pl.BlockSpec((m, k_total), lambda j: (0, 0)),
                  pl.BlockSpec((None, k_total, bn), lambda j: (layer, 0, j))],
        out_specs=pl.BlockSpec((m, bn), lambda j: (0, j)),
        compiler_params=_params("parallel"),
        name="mm_rows",
    )(a, w)


def _pool_kernel(u_ref, halo_ref, w_ref, scale_ref, o_ref, *, ts):
    i = pl.program_id(1)
    cur = u_ref[0].astype(jnp.float32)
    halo = halo_ref[0].astype(jnp.float32)
    halo = jnp.where(i > 0, halo, 0.0)
    t1 = (i * ts + 1 + lax.broadcasted_iota(jnp.int32, (ts, 1), 0)).astype(jnp.float32)
    for g, win in enumerate(POOL_WINDOWS):
        c0, c1 = g * POOL_GROUP, (g + 1) * POOL_GROUP
        x = jnp.concatenate([halo[:, c0:c1], cur[:, c0:c1]], axis=0)
        s = x
        d = 1
        while d < win:
            s = s[d:] + s[:-d]
            d *= 2
        s = s[POOL_HALO - (win - 1):]
        mean = s / jnp.minimum(t1, float(win))
        p = (mean - cur[:, c0:c1]).astype(jnp.bfloat16)
        y = jnp.dot(p, w_ref[g], preferred_element_type=jnp.float32)
        o_ref[0, :, c0:c1] = (y * scale_ref[:, c0:c1]).astype(o_ref.dtype)


def pool_mixer(proj, w_group, scale, ts=512):
    b, s, _ = proj.shape
    hb = ts // POOL_HALO
    return pl.pallas_call(
        functools.partial(_pool_kernel, ts=ts),
        out_shape=jax.ShapeDtypeStruct((b, s, SELF_W), jnp.bfloat16),
        grid=(b, s // ts),
        in_specs=[
            pl.BlockSpec((1, ts, SELF_W), lambda bi, i: (bi, i, 0)),
            pl.BlockSpec((1, POOL_HALO, SELF_W), lambda bi, i: (bi, jnp.maximum(i * hb - 1, 0), 0)),
            pl.BlockSpec((len(POOL_WINDOWS), POOL_GROUP, POOL_GROUP), lambda bi, i: (0, 0, 0)),
            pl.BlockSpec((1, SELF_W), lambda bi, i: (0, 0)),
        ],
        out_specs=pl.BlockSpec((1, ts, SELF_W), lambda bi, i: (bi, i, 0)),
        compiler_params=_params("parallel", "parallel"),
        name="pool_mixer",
    )(proj, proj, w_group, scale.reshape(1, SELF_W))


def _xattn_kernel(q_ref, k_ref, v_ref, o_ref):
    q = q_ref[0]
    k = k_ref[0]
    v = v_ref[0]
    s = lax.dot_general(q, k, (((1,), (1,)), ((), ())), preferred_element_type=jnp.float32)
    s = s * (XA_HEAD_DIM ** -0.5)
    m = jnp.max(s, axis=-1, keepdims=True)
    e = jnp.exp(s - m)
    l = jnp.sum(e, axis=-1, keepdims=True)
    o = jnp.dot(e.astype(jnp.bfloat16), v, preferred_element_type=jnp.float32)
    o_ref[0] = (o / l).astype(o_ref.dtype)


def mem_xattn(proj, q_col0, mem_kv, ts=1024):
    b, s, _ = proj.shape
    qb = q_col0 // XA_HEAD_DIM
    assert qb * XA_HEAD_DIM == q_col0
    return pl.pallas_call(
        _xattn_kernel,
        out_shape=jax.ShapeDtypeStruct((b, s, XA_W), jnp.bfloat16),
        grid=(b, XA_HEADS, s // ts),
        in_specs=[
            pl.BlockSpec((1, ts, XA_HEAD_DIM), lambda bi, h, i: (bi, i, qb + h)),
            pl.BlockSpec((1, MEM_LEN, XA_HEAD_DIM), lambda bi, h, i: (bi, 0, h)),
            pl.BlockSpec((1, MEM_LEN, XA_HEAD_DIM), lambda bi, h, i: (bi, 0, XA_HEADS + h)),
        ],
        out_specs=pl.BlockSpec((1, ts, XA_HEAD_DIM), lambda bi, h, i: (bi, i, h)),
        compiler_params=_params("parallel", "parallel", "parallel"),
        name="mem_xattn",
    )(proj, mem_kv, mem_kv)


def _rope_table_kernel(pos_ref, freq_ref, sign_ref, cos_ref, sin_ref):
    ang = pos_ref[0].astype(jnp.float32) * freq_ref[...]
    cos_ref[0] = jnp.cos(ang)
    sin_ref[0] = jnp.sin(ang) * sign_ref[...]


def rope_tables(positions, ts=1024):
    b, s = positions.shape
    half = ROT_DIM // 2
    inv_freq = ROPE_THETA ** (-jnp.arange(0, ROT_DIM, 2, dtype=jnp.float32) / ROT_DIM)
    freq = jnp.concatenate([inv_freq, inv_freq, jnp.zeros((HEAD_DIM - ROT_DIM,), jnp.float32)]).reshape(1, HEAD_DIM)
    sign = jnp.concatenate([-jnp.ones((half,), jnp.float32), jnp.ones((half,), jnp.float32),
                            jnp.zeros((HEAD_DIM - ROT_DIM,), jnp.float32)]).reshape(1, HEAD_DIM)
    out = jax.ShapeDtypeStruct((b, s, HEAD_DIM), jnp.float32)
    return pl.pallas_call(
        _rope_table_kernel,
        out_shape=(out, out),
        grid=(b, s // ts),
        in_specs=[pl.BlockSpec((1, ts, 1), lambda bi, i: (bi, i, 0)),
                  pl.BlockSpec((1, HEAD_DIM), lambda bi, i: (0, 0)),
                  pl.BlockSpec((1, HEAD_DIM), lambda bi, i: (0, 0))],
        out_specs=(pl.BlockSpec((1, ts, HEAD_DIM), lambda bi, i: (bi, i, 0)),
                   pl.BlockSpec((1, ts, HEAD_DIM), lambda bi, i: (bi, i, 0))),
        compiler_params=_params("parallel", "parallel"),
        name="rope_tables",
    )(positions.reshape(b, s, 1), freq, sign)


def _rope(xb, cos_t, sin_t, rot):
    partner = jnp.dot(xb, rot, preferred_element_type=jnp.float32)
    return xb.astype(jnp.float32) * cos_t + partner * sin_t


def _swa_kernel(sink_ref, q_ref, kc_ref, vc_ref, kp_ref, vp_ref, cosc_ref, sinc_ref, cosp_ref, sinp_ref, rot_ref,
                o_ref):
    n = pl.program_id(1)
    blk = WINDOW
    cols = GQA_GROUP * blk
    cos_c, sin_c = cosc_ref[0], sinc_ref[0]
    cos_p, sin_p = cosp_ref[0], sinp_ref[0]
    rot = rot_ref[...]

    kj = lax.broadcasted_iota(jnp.int32, (blk, cols), 0)
    col = lax.broadcasted_iota(jnp.int32, (blk, cols), 1)
    cur_live = kj <= col % blk
    live = jnp.logical_or(cur_live, n > 0)
    col_group = lax.broadcasted_iota(jnp.int32, (1, cols), 1) // blk

    for h in range(N_KV_HEADS):
        hs = slice(h * HEAD_DIM, (h + 1) * HEAD_DIM)
        kp = _rope(kp_ref[0, :, hs], cos_p, sin_p, rot)
        kc = _rope(kc_ref[0, :, hs], cos_c, sin_c, rot)
        k = jnp.concatenate([kp, kc], axis=0).astype(jnp.bfloat16)
        v = jnp.concatenate([vp_ref[0, :, hs], vc_ref[0, :, hs]], axis=0)
        q_parts = []
        sink = jnp.zeros((1, cols), jnp.float32)
        for g in range(GQA_GROUP):
            c0 = (h * GQA_GROUP + g) * HEAD_DIM
            q_parts.append(_rope(q_ref[0, :, c0:c0 + HEAD_DIM], cos_c, sin_c, rot))
            sink = jnp.where(col_group == g, sink_ref[h * GQA_GROUP + g], sink)
        q = jnp.concatenate(q_parts, axis=0).astype(jnp.bfloat16)
        st = lax.dot_general(k, q, (((1,), (1,)), ((), ())), preferred_element_type=jnp.float32)
        s = jnp.where(cur_live, st[blk:], st[:blk]) * (HEAD_DIM ** -0.5)
        s = jnp.where(live, s, NEG)
        m = jnp.maximum(jnp.max(s, axis=0, keepdims=True), sink)
        e = jnp.exp(s - m)
        l = jnp.sum(e, axis=0, keepdims=True) + jnp.exp(sink - m)
        p = e * (1.0 / l)
        pt = jnp.concatenate([jnp.where(cur_live, 0.0, p), jnp.where(cur_live, p, 0.0)], axis=0)
        o = lax.dot_general(pt.astype(jnp.bfloat16), v, (((0,), (0,)), ((), ())),
                            preferred_element_type=jnp.float32)
        for g in range(GQA_GROUP):
            c0 = (h * GQA_GROUP + g) * HEAD_DIM
            o_ref[0, :, c0:c0 + HEAD_DIM] = o[g * blk:(g + 1) * blk].astype(o_ref.dtype)


def swa_attention(q, kvx, sink, cos_t, sin_t):
    b, s, _ = q.shape
    blk = WINDOW
    half = ROT_DIM // 2
    src_lane = lax.broadcasted_iota(jnp.int32, (HEAD_DIM, HEAD_DIM), 0)
    dst_lane = lax.broadcasted_iota(jnp.int32, (HEAD_DIM, HEAD_DIM), 1)
    partner_of_dst = jnp.where(dst_lane < half, dst_lane + half, dst_lane - half)
    rot = jnp.logical_and(dst_lane < ROT_DIM, src_lane == partner_of_dst).astype(jnp.bfloat16)
    cur = lambda bi, n, sk: (bi, n, 0)
    prev = lambda bi, n, sk: (bi, jnp.maximum(n - 1, 0), 0)
    grid_spec = pltpu.PrefetchScalarGridSpec(
        num_scalar_prefetch=1,
        grid=(b, s // blk),
        in_specs=[
            pl.BlockSpec((1, blk, SELF_W), cur),
            pl.BlockSpec((1, blk, KV_W), cur),
            pl.BlockSpec((1, blk, KV_W), lambda bi, n, sk: (bi, n, 1)),
            pl.BlockSpec((1, blk, KV_W), prev),
            pl.BlockSpec((1, blk, KV_W), lambda bi, n, sk: (bi, jnp.maximum(n - 1, 0), 1)),
            pl.BlockSpec((1, blk, HEAD_DIM), cur),
            pl.BlockSpec((1, blk, HEAD_DIM), cur),
            pl.BlockSpec((1, blk, HEAD_DIM), prev),
            pl.BlockSpec((1, blk, HEAD_DIM), prev),
            pl.BlockSpec((HEAD_DIM, HEAD_DIM), lambda bi, n, sk: (0, 0)),
        ],
        out_specs=pl.BlockSpec((1, blk, SELF_W), cur),
    )
    return pl.pallas_call(
        _swa_kernel,
        out_shape=jax.ShapeDtypeStruct((b, s, SELF_W), jnp.bfloat16),
        grid_spec=grid_spec,
        compiler_params=_params("parallel", "parallel"),
        name="swa_attention",
    )(sink, q, kvx, kvx, kvx, kvx, cos_t, sin_t, cos_t, sin_t, rot)


def kernel(x, mem, positions, norm_mix, norm_mem, norm_mlp, w_mem_kv, pool_w_in, pool_w_group, pool_scale,
           pool_w_out, attn_w_in, attn_sink, attn_w_out, mlp_w1, mlp_w2, final_norm):
    b, s, d = x.shape
    t = b * s
    bf16 = jnp.bfloat16
    h = x.reshape(t, d)
    mem2 = mem.reshape(b * MEM_LEN, d)
    cos_t, sin_t = rope_tables(positions)

    hb = pss = None
    for i in range(2):
        memn = rmsnorm(mem2, norm_mem[i], bf16)
        mem_kv = matmul_rows(memn, w_mem_kv, i).reshape(b, MEM_LEN, 2 * XA_W)
        if i == 0:
            hn = rmsnorm(h, norm_mix[0], bf16)
            proj = matmul_wres([hn], pool_w_in, 0, SELF_W + XA_W).reshape(b, s, -1)
            y_self = pool_mixer(proj, pool_w_group[0].astype(bf16), pool_scale[0])
            y_mem = mem_xattn(proj, SELF_W, mem_kv)
            w_out = pool_w_out
        else:
            norm = (norm_mix[i], pss)
            q = matmul_wres([hb], attn_w_in, 0, SELF_W, norm=norm).reshape(b, s, -1)
            kvx = matmul_wres([hb], attn_w_in[:, :, SELF_W:], 0, 2 * KV_W + XA_W, bn=2 * KV_W + XA_W, norm=norm)
            kvx = kvx.reshape(b, s, -1)
            y_self = swa_attention(q, kvx, attn_sink[0], cos_t, sin_t)
            y_mem = mem_xattn(kvx, 2 * KV_W, mem_kv)
            w_out = attn_w_out
        h, hb, pss = matmul_wres([y_self.reshape(t, SELF_W), y_mem.reshape(t, XA_W)], w_out, 0, d,
                                 epilogue="residual", res=h, out_dtype=jnp.float32, bm=512, emit_norm=True)
        a, w2b = matmul_wres([hb], mlp_w1, i, mlp_w1.shape[2], epilogue="relu2", norm=(norm_mlp[i], pss),
                             side=(mlp_w2, i))
        if i == 0:
            h, hb, pss = matmul_kacc(a, w2b, h, emit_norm=True)
        else:
            h = matmul_kacc(a, w2b, h)
    return rmsnorm(h, final_norm, jnp.float32).reshape(b, s, d)
```

```python
import functools

import jax
import jax.numpy as jnp
from jax import lax
from jax.experimental import pallas as pl
from jax.experimental.pallas import tpu as pltpu

D_MODEL = 4096
SELF_W = 3072
XA_W = 1024
XA_HEADS = 4
XA_HEAD_DIM = 256
MEM_LEN = 256
POOL_WINDOWS = (2, 4, 8, 16)
POOL_GROUP = 768
HEAD_DIM = 128
N_Q_HEADS = 24
GQA_GROUP = 8
N_KV_HEADS = 3
KV_W = 384
WINDOW = 128
ROT_DIM = 32
ROPE_THETA = 500000.0
EPS = 1e-6
NEG = -1e30

VMEM_LIMIT_BYTES = 56 * 1024 * 1024
LANES = 128
MM_ROW_CHUNK = 512
POOL_HALO = 16


def _params(*sem):
    return pltpu.CompilerParams(dimension_semantics=sem, vmem_limit_bytes=VMEM_LIMIT_BYTES)


def _rmsnorm_kernel(x_ref, g_ref, o_ref):
    x = x_ref[...]
    ms = jnp.mean(x * x, axis=-1, keepdims=True)
    o_ref[...] = (x * lax.rsqrt(ms + EPS) * g_ref[...]).astype(o_ref.dtype)


def rmsnorm(x, g, out_dtype, rows=512):
    r, d = x.shape
    return pl.pallas_call(
        _rmsnorm_kernel,
        out_shape=jax.ShapeDtypeStruct((r, d), out_dtype),
        grid=(r // rows,),
        in_specs=[pl.BlockSpec((rows, d), lambda i: (i, 0)),
                  pl.BlockSpec((1, d), lambda i: (0, 0))],
        out_specs=pl.BlockSpec((rows, d), lambda i: (i, 0)),
        compiler_params=_params("parallel"),
        name="rmsnorm",
    )(x, g.reshape(1, d))


def _epilogue(acc, epilogue, res_ref, o_ref, rs):
    if epilogue == "relu2":
        acc = jnp.square(jnp.maximum(acc, 0.0))
    elif epilogue == "residual":
        acc = res_ref[rs, :] + acc
    o_ref[rs, :] = acc.astype(o_ref.dtype)
    return acc


def _lane_group_sumsq(h_new):
    sq = h_new * h_new
    pss = sq[:, :LANES]
    for c in range(1, sq.shape[1] // LANES):
        pss = pss + sq[:, c * LANES:(c + 1) * LANES]
    return pss


def _row_rstd(pss_ref, d):
    return lax.rsqrt(jnp.sum(pss_ref[...], axis=-1, keepdims=True) * (1.0 / d) + EPS)


def _mm_wres_kernel(*refs, n_a, k_splits, chunk, n_j, row_chunk, epilogue, normed, emit_norm, side):
    refs = list(refs)
    a_refs = [refs.pop(0) for _ in range(n_a)]
    w_ref = refs.pop(0)
    g_ref = refs.pop(0) if normed else None
    pss_in_ref = refs.pop(0) if normed else None
    res_ref = refs.pop(0) if epilogue == "residual" else None
    side_in_ref = refs.pop(0) if side else None
    o_ref = refs.pop(0)
    hb_ref = refs.pop(0) if emit_norm else None
    pss_out_ref = refs.pop(0) if emit_norm else None
    side_out_ref = refs.pop(0) if side else None
    (wbf_ref,) = refs
    jp = pl.program_id(0)
    i = pl.program_id(1)
    n_slots = wbf_ref.shape[0]

    @pl.when(jp < n_j)
    def _():
        row0 = pl.multiple_of(i * chunk, chunk)
        w = w_ref[...]
        if normed:
            w = w * g_ref[...]
        wbf_ref[jp % n_slots, pl.ds(row0, chunk), :] = w.astype(jnp.bfloat16)

    @pl.when(jp > 0)
    def _():
        use_slot = (jp - 1) % n_slots
        rstd = _row_rstd(pss_in_ref, wbf_ref.shape[1]) if normed else None
        for r0 in range(0, o_ref.shape[0], row_chunk):
            rs = slice(r0, r0 + row_chunk)
            acc = None
            off = 0
            for a_ref, kw in zip(a_refs, k_splits):
                part = jnp.dot(a_ref[rs, :], wbf_ref[use_slot, pl.ds(off, kw), :], preferred_element_type=jnp.float32)
                acc = part if acc is None else acc + part
                off += kw
            if normed:
                acc = acc * rstd[rs]
            total = _epilogue(acc, epilogue, res_ref, o_ref, rs)
            if emit_norm:
                hb_ref[rs, :] = total.astype(hb_ref.dtype)
                pss_out_ref[rs, :] = _lane_group_sumsq(total)
        if side:
            side_out_ref[...] = side_in_ref[...].astype(side_out_ref.dtype)


def matmul_wres(a_list, w, layer, n_out, *, epilogue="cast", res=None, out_dtype=jnp.bfloat16, bm=1024, bn=1024,
                norm=None, emit_norm=False, side=None):
    m = a_list[0].shape[0]
    k_total = w.shape[1]
    k_splits = tuple(a.shape[1] for a in a_list)
    n_i, n_j = m // bm, n_out // bn
    chunk = k_total // n_i
    assert sum(k_splits) == k_total and n_i * bm == m and n_j * bn == n_out and chunk * n_i == k_total
    assert chunk % 16 == 0 and (not emit_norm or epilogue == "residual")

    def row_blk(jp, i):
        return jnp.where(jp == 0, 0, i)

    def stage_blk(jp, i):
        return jnp.where(jp < n_j, i, 0)

    o_map = lambda jp, i: (row_blk(jp, i), jnp.maximum(jp - 1, 0))
    in_specs = [pl.BlockSpec((bm, kw), lambda jp, i: (row_blk(jp, i), 0)) for kw in k_splits]
    in_specs.append(pl.BlockSpec((None, chunk, bn), lambda jp, i: (layer, stage_blk(jp, i), jnp.minimum(jp, n_j - 1))))
    args = list(a_list) + [w]
    if norm is not None:
        gain, pss = norm
        in_specs.append(pl.BlockSpec((chunk, 1), lambda jp, i: (stage_blk(jp, i), 0)))
        in_specs.append(pl.BlockSpec((bm, pss.shape[1]), lambda jp, i: (row_blk(jp, i), 0)))
        args += [gain.reshape(k_total, 1), pss]
    if epilogue == "residual":
        in_specs.append(pl.BlockSpec((bm, bn), o_map))
        args.append(res)
    out_shape = [jax.ShapeDtypeStruct((m, n_out), out_dtype)]
    out_specs = [pl.BlockSpec((bm, bn), o_map)]
    if emit_norm:
        out_shape += [jax.ShapeDtypeStruct((m, n_out), jnp.bfloat16),
                      jax.ShapeDtypeStruct((m, n_j * LANES), jnp.float32)]
        out_specs += [pl.BlockSpec((bm, bn), o_map), pl.BlockSpec((bm, LANES), o_map)]
    if side is not None:
        w_side, layer2 = side
        _, k2, n2 = w_side.shape
        rows2 = k2 // (n_j * n_i)
        assert rows2 * n_j * n_i == k2 and rows2 % 16 == 0
        slab = lambda jp, i: jnp.where(jp == 0, 0, (jp - 1) * n_i + i)
        in_specs.append(pl.BlockSpec((None, rows2, n2), lambda jp, i: (layer2, slab(jp, i), 0)))
        args.append(w_side)
        out_shape.append(jax.ShapeDtypeStruct((k2, n2), jnp.bfloat16))
        out_specs.append(pl.BlockSpec((rows2, n2), lambda jp, i: (slab(jp, i), 0)))
    outs = pl.pallas_call(
        functools.partial(_mm_wres_kernel, n_a=len(a_list), k_splits=k_splits, chunk=chunk, n_j=n_j,
                          row_chunk=min(bm, MM_ROW_CHUNK), epilogue=epilogue,
                          normed=norm is not None, emit_norm=emit_norm, side=side is not None),
        out_shape=out_shape,
        grid=(n_j + 1, n_i),
        in_specs=in_specs,
        out_specs=out_specs,
        scratch_shapes=[pltpu.VMEM((min(2, n_j), k_total, bn), jnp.bfloat16)],
        compiler_params=_params("arbitrary", "arbitrary"),
        name="mm_wres_" + epilogue,
    )(*args)
    return outs[0] if len(outs) == 1 else tuple(outs)


def _mm_kacc_kernel(a_ref, w_ref, res_ref, o_ref, *norm_refs, nk):
    k = pl.program_id(2)

    @pl.when(k == 0)
    def _():
        o_ref[...] = res_ref[...]

    def partial_product():
        return jnp.dot(a_ref[...], w_ref[...], preferred_element_type=jnp.float32)

    if not norm_refs:
        o_ref[...] += partial_product()
    else:
        @pl.when(k < nk - 1)
        def _():
            o_ref[...] += partial_product()

        @pl.when(k == nk - 1)
        def _():
            total = o_ref[...] + partial_product()
            o_ref[...] = total
            hb_ref, pss_ref = norm_refs
            hb_ref[...] = total.astype(hb_ref.dtype)
            pss_ref[...] = _lane_group_sumsq(total)


def matmul_kacc(a, w, res, *, emit_norm=False, bm=1024, bn=1024, bk=2048):
    m, k_total = a.shape
    n = w.shape[1]
    assert m % bm == 0 and n % bn == 0 and k_total % bk == 0
    o_map = lambda j, i, k: (i, j)
    out_shape = [jax.ShapeDtypeStruct((m, n), jnp.float32)]
    out_specs = [pl.BlockSpec((bm, bn), o_map)]
    if emit_norm:
        out_shape += [jax.ShapeDtypeStruct((m, n), jnp.bfloat16),
                      jax.ShapeDtypeStruct((m, (n // bn) * LANES), jnp.float32)]
        out_specs += [pl.BlockSpec((bm, bn), o_map), pl.BlockSpec((bm, LANES), o_map)]
    outs = pl.pallas_call(
        functools.partial(_mm_kacc_kernel, nk=k_total // bk),
        out_shape=out_shape,
        grid=(n // bn, m // bm, k_total // bk),
        in_specs=[pl.BlockSpec((bm, bk), lambda j, i, k: (i, k)),
                  pl.BlockSpec((bk, bn), lambda j, i, k: (k, j)),
                  pl.BlockSpec((bm, bn), o_map)],
        out_specs=out_specs,
        compiler_params=_params("parallel", "parallel", "arbitrary"),
        name="mm_kacc_residual",
    )(a, w, res)
    return outs[0] if len(outs) == 1 else tuple(outs)


def _mm_rows_kernel(a_ref, w_ref, o_ref):
    o_ref[...] = jnp.dot(a_ref[...], w_ref[...].astype(jnp.bfloat16),
                         preferred_element_type=jnp.float32).astype(o_ref.dtype)


def matmul_rows(a, w, layer, *, bn=512):
    m, k_total = a.shape
    n = w.shape[2]
    assert n % bn == 0
    return pl.pallas_call(
        _mm_rows_kernel,
        out_shape=jax.ShapeDtypeStruct((m, n), jnp.bfloat16),
        grid=(n // bn,),
        in_specs=[pl.BlockSpec((m, k_total), lambda j: (0, 0)),
                  pl.BlockSpec((None, k_total, bn), lambda j: (layer, 0, j))],
        out_specs=pl.BlockSpec((m, bn), lambda j: (0, j)),
        compiler_params=_params("parallel"),
        name="mm_rows",
    )(a, w)


def _pool_kernel(u_ref, halo_ref, w_ref, scale_ref, o_ref, *, ts):
    i = pl.program_id(1)
    cur = u_ref[0].astype(jnp.float32)
    halo = halo_ref[0].astype(jnp.float32)
    halo = jnp.where(i > 0, halo, 0.0)
    t1 = (i * ts + 1 + lax.broadcasted_iota(jnp.int32, (ts, 1), 0)).astype(jnp.float32)
    for g, win in enumerate(POOL_WINDOWS):
        c0, c1 = g * POOL_GROUP, (g + 1) * POOL_GROUP
        x = jnp.concatenate([halo[:, c0:c1], cur[:, c0:c1]], axis=0)
        s = x
        d = 1
        while d < win:
            s = s[d:] + s[:-d]
            d *= 2
        s = s[POOL_HALO - (win - 1):]
        mean = s / jnp.minimum(t1, float(win))
        p = (mean - cur[:, c0:c1]).astype(jnp.bfloat16)
        y = jnp.dot(p, w_ref[g], preferred_element_type=jnp.float32)
        o_ref[0, :, c0:c1] = (y * scale_ref[:, c0:c1]).astype(o_ref.dtype)


def pool_mixer(proj, w_group, scale, ts=512):
    b, s, _ = proj.shape
    hb = ts // POOL_HALO
    return pl.pallas_call(
        functools.partial(_pool_kernel, ts=ts),
        out_shape=jax.ShapeDtypeStruct((b, s, SELF_W), jnp.bfloat16),
        grid=(b, s // ts),
        in_specs=[
            pl.BlockSpec((1, ts, SELF_W), lambda bi, i: (bi, i, 0)),
            pl.BlockSpec((1, POOL_HALO, SELF_W), lambda bi, i: (bi, jnp.maximum(i * hb - 1, 0), 0)),
            pl.BlockSpec((len(POOL_WINDOWS), POOL_GROUP, POOL_GROUP), lambda bi, i: (0, 0, 0)),
            pl.BlockSpec((1, SELF_W), lambda bi, i: (0, 0)),
        ],
        out_specs=pl.BlockSpec((1, ts, SELF_W), lambda bi, i: (bi, i, 0)),
        compiler_params=_params("parallel", "parallel"),
        name="pool_mixer",
    )(proj, proj, w_group, scale.reshape(1, SELF_W))


def _xattn_kernel(*refs):
    q_refs, kv_ref, o_ref = refs[:XA_HEADS], refs[XA_HEADS], refs[XA_HEADS + 1]
    for h in range(XA_HEADS):
        hs = slice(h * XA_HEAD_DIM, (h + 1) * XA_HEAD_DIM)
        q = q_refs[h][0]
        k = kv_ref[0, :, hs]
        v = kv_ref[0, :, XA_W + h * XA_HEAD_DIM:XA_W + (h + 1) * XA_HEAD_DIM]
        s = lax.dot_general(q, k, (((1,), (1,)), ((), ())), preferred_element_type=jnp.float32)
        s = s * (XA_HEAD_DIM ** -0.5)
        m = jnp.max(s, axis=-1, keepdims=True)
        e = jnp.exp(s - m)
        l = jnp.sum(e, axis=-1, keepdims=True)
        o = jnp.dot(e.astype(jnp.bfloat16), v, preferred_element_type=jnp.float32)
        o_ref[0, :, hs] = (o / l).astype(o_ref.dtype)


def mem_xattn(proj, q_col0, mem_kv, ts=1024):
    b, s, _ = proj.shape
    qb = q_col0 // XA_HEAD_DIM
    assert qb * XA_HEAD_DIM == q_col0
    q_specs = [pl.BlockSpec((1, ts, XA_HEAD_DIM), functools.partial(lambda bi, i, h: (bi, i, qb + h), h=h))
               for h in range(XA_HEADS)]
    return pl.pallas_call(
        _xattn_kernel,
        out_shape=jax.ShapeDtypeStruct((b, s, XA_W), jnp.bfloat16),
        grid=(b, s // ts),
        in_specs=q_specs + [pl.BlockSpec((1, MEM_LEN, 2 * XA_W), lambda bi, i: (bi, 0, 0))],
        out_specs=pl.BlockSpec((1, ts, XA_W), lambda bi, i: (bi, i, 0)),
        compiler_params=_params("parallel", "parallel"),
        name="mem_xattn",
    )(*([proj] * XA_HEADS), mem_kv)


def _rope_table_kernel(pos_ref, freq_ref, sign_ref, cos_ref, sin_ref):
    ang = pos_ref[0].astype(jnp.float32) * freq_ref[...]
    cos_ref[0] = jnp.cos(ang)
    sin_ref[0] = jnp.sin(ang) * sign_ref[...]


def rope_tables(positions, ts=1024):
    b, s = positions.shape
    half = ROT_DIM // 2
    inv_freq = ROPE_THETA ** (-jnp.arange(0, ROT_DIM, 2, dtype=jnp.float32) / ROT_DIM)
    freq = jnp.concatenate([inv_freq, inv_freq, jnp.zeros((HEAD_DIM - ROT_DIM,), jnp.float32)]).reshape(1, HEAD_DIM)
    sign = jnp.concatenate([-jnp.ones((half,), jnp.float32), jnp.ones((half,), jnp.float32),
                            jnp.zeros((HEAD_DIM - ROT_DIM,), jnp.float32)]).reshape(1, HEAD_DIM)
    out = jax.ShapeDtypeStruct((b, s, HEAD_DIM), jnp.float32)
    return pl.pallas_call(
        _rope_table_kernel,
        out_shape=(out, out),
        grid=(b, s // ts),
        in_specs=[pl.BlockSpec((1, ts, 1), lambda bi, i: (bi, i, 0)),
                  pl.BlockSpec((1, HEAD_DIM), lambda bi, i: (0, 0)),
                  pl.BlockSpec((1, HEAD_DIM), lambda bi, i: (0, 0))],
        out_specs=(pl.BlockSpec((1, ts, HEAD_DIM), lambda bi, i: (bi, i, 0)),
                   pl.BlockSpec((1, ts, HEAD_DIM), lambda bi, i: (bi, i, 0))),
        compiler_params=_params("parallel", "parallel"),
        name="rope_tables",
    )(positions.reshape(b, s, 1), freq, sign)


def _rope(xb, cos_t, sin_t, rot):
    partner = jnp.dot(xb, rot, preferred_element_type=jnp.float32)
    return xb.astype(jnp.float32) * cos_t + partner * sin_t


def _swa_kernel(sink_ref, q_ref, kc_ref, vc_ref, kp_ref, vp_ref, cosc_ref, sinc_ref, cosp_ref, sinp_ref, rot_ref,
                o_ref):
    n = pl.program_id(1)
    blk = WINDOW
    cols = GQA_GROUP * blk
    cos_c, sin_c = cosc_ref[0], sinc_ref[0]
    cos_p, sin_p = cosp_ref[0], sinp_ref[0]
    rot = rot_ref[...]

    kj = lax.broadcasted_iota(jnp.int32, (blk, cols), 0)
    col = lax.broadcasted_iota(jnp.int32, (blk, cols), 1)
    cur_live = kj <= col % blk
    live = jnp.logical_or(cur_live, n > 0)
    col_group = lax.broadcasted_iota(jnp.int32, (1, cols), 1) // blk

    for h in range(N_KV_HEADS):
        hs = slice(h * HEAD_DIM, (h + 1) * HEAD_DIM)
        kp = _rope(kp_ref[0, :, hs], cos_p, sin_p, rot)
        kc = _rope(kc_ref[0, :, hs], cos_c, sin_c, rot)
        k = jnp.concatenate([kp, kc], axis=0).astype(jnp.bfloat16)
        v = jnp.concatenate([vp_ref[0, :, hs], vc_ref[0, :, hs]], axis=0)
        q_parts = []
        sink = jnp.zeros((1, cols), jnp.float32)
        for g in range(GQA_GROUP):
            c0 = (h * GQA_GROUP + g) * HEAD_DIM
            q_parts.append(_rope(q_ref[0, :, c0:c0 + HEAD_DIM], cos_c, sin_c, rot))
            sink = jnp.where(col_group == g, sink_ref[h * GQA_GROUP + g], sink)
        q = jnp.concatenate(q_parts, axis=0).astype(jnp.bfloat16)
        st = lax.dot_general(k, q, (((1,), (1,)), ((), ())), preferred_element_type=jnp.float32)
        s = jnp.where(cur_live, st[blk:], st[:blk]) * (HEAD_DIM ** -0.5)
        s = jnp.where(live, s, NEG)
        m = jnp.maximum(jnp.max(s, axis=0, keepdims=True), sink)
        e = jnp.exp(s - m)
        l = jnp.sum(e, axis=0, keepdims=True) + jnp.exp(sink - m)
        p = e * (1.0 / l)
        pt = jnp.concatenate([jnp.where(cur_live, 0.0, p), jnp.where(cur_live, p, 0.0)], axis=0)
        o = lax.dot_general(pt.astype(jnp.bfloat16), v, (((0,), (0,)), ((), ())),
                            preferred_element_type=jnp.float32)
        for g in range(GQA_GROUP):
            c0 = (h * GQA_GROUP + g) * HEAD_DIM
            o_ref[0, :, c0:c0 + HEAD_DIM] = o[g * blk:(g + 1) * blk].astype(o_ref.dtype)


def swa_attention(q, kvx, sink, cos_t, sin_t):
    b, s, _ = q.shape
    blk = WINDOW
    half = ROT_DIM // 2
    src_lane = lax.broadcasted_iota(jnp.int32, (HEAD_DIM, HEAD_DIM), 0)
    dst_lane = lax.broadcasted_iota(jnp.int32, (HEAD_DIM, HEAD_DIM), 1)
    partner_of_dst = jnp.where(dst_lane < half, dst_lane + half, dst_lane - half)
    rot = jnp.logical_and(dst_lane < ROT_DIM, src_lane == partner_of_dst).astype(jnp.bfloat16)
    cur = lambda bi, n, sk: (bi, n, 0)
    prev = lambda bi, n, sk: (bi, jnp.maximum(n - 1, 0), 0)
    grid_spec = pltpu.PrefetchScalarGridSpec(
        num_scalar_prefetch=1,
        grid=(b, s // blk),
        in_specs=[
            pl.BlockSpec((1, blk, SELF_W), cur),
            pl.BlockSpec((1, blk, KV_W), cur),
            pl.BlockSpec((1, blk, KV_W), lambda bi, n, sk: (bi, n, 1)),
            pl.BlockSpec((1, blk, KV_W), prev),
            pl.BlockSpec((1, blk, KV_W), lambda bi, n, sk: (bi, jnp.maximum(n - 1, 0), 1)),
            pl.BlockSpec((1, blk, HEAD_DIM), cur),
            pl.BlockSpec((1, blk, HEAD_DIM), cur),
            pl.BlockSpec((1, blk, HEAD_DIM), prev),
            pl.BlockSpec((1, blk, HEAD_DIM), prev),
            pl.BlockSpec((HEAD_DIM, HEAD_DIM), lambda bi, n, sk: (0, 0)),
        ],
        out_specs=pl.BlockSpec((1, blk, SELF_W), cur),
    )
    return pl.pallas_call(
        _swa_kernel,
        out_shape=jax.ShapeDtypeStruct((b, s, SELF_W), jnp.bfloat16),
        grid_spec=grid_spec,
        compiler_params=_params("parallel", "parallel"),
        name="swa_attention",
    )(sink, q, kvx, kvx, kvx, kvx, cos_t, sin_t, cos_t, sin_t, rot)


def kernel(x, mem, positions, norm_mix, norm_mem, norm_mlp, w_mem_kv, pool_w_in, pool_w_group, pool_scale,
           pool_w_out, attn_w_in, attn_sink, attn_w_out, mlp_w1, mlp_w2, final_norm):
    b, s, d = x.shape
    t = b * s
    bf16 = jnp.bfloat16
    h = x.reshape(t, d)
    mem2 = mem.reshape(b * MEM_LEN, d)
    cos_t, sin_t = rope_tables(positions)

    hb = pss = None
    for i in range(2):
        memn = rmsnorm(mem2, norm_mem[i], bf16)
        mem_kv = matmul_rows(memn, w_mem_kv, i).reshape(b, MEM_LEN, 2 * XA_W)
        if i == 0:
            hn = rmsnorm(h, norm_mix[0], bf16)
            proj = matmul_wres([hn], pool_w_in, 0, SELF_W + XA_W).reshape(b, s, -1)
            y_self = pool_mixer(proj, pool_w_group[0].astype(bf16), pool_scale[0])
            y_mem = mem_xattn(proj, SELF_W, mem_kv)
            w_out = pool_w_out
        else:
            norm = (norm_mix[i], pss)
            q = matmul_wres([hb], attn_w_in, 0, SELF_W, norm=norm).reshape(b, s, -1)
            kvx = matmul_wres([hb], attn_w_in[:, :, SELF_W:], 0, 2 * KV_W + XA_W, bn=2 * KV_W + XA_W, norm=norm)
            kvx = kvx.reshape(b, s, -1)
            y_self = swa_attention(q, kvx, attn_sink[0], cos_t, sin_t)
            y_mem = mem_xattn(kvx, 2 * KV_W, mem_kv)
            w_out = attn_w_out
        h, hb, pss = matmul_wres([y_self.reshape(t, SELF_W), y_mem.reshape(t, XA_W)], w_out, 0, d,
                                 epilogue="residual", res=h, out_dtype=jnp.float32, bm=512, emit_norm=True)
        a, w2b = matmul_wres([hb], mlp_w1, i, mlp_w1.shape[2], epilogue="relu2", norm=(norm_mlp[i], pss),
                             side=(mlp_w2, i))
        if i == 0:
            h, hb, pss = matmul_kacc(a, w2b, h, emit_norm=True)
        else:
            h = matmul_kacc(a, w2b, h, bm=2048, bk=1024)
    return rmsnorm(h, final_norm, jnp.float32).reshape(b, s, d)
```

```python
import functools

import jax
import jax.numpy as jnp
from jax import lax
from jax.experimental import pallas as pl
from jax.experimental.pallas import tpu as pltpu

D_MODEL = 4096
SELF_W = 3072
XA_W = 1024
XA_HEADS = 4
XA_HEAD_DIM = 256
MEM_LEN = 256
POOL_WINDOWS = (2, 4, 8, 16)
POOL_GROUP = 768
HEAD_DIM = 128
N_Q_HEADS = 24
GQA_GROUP = 8
N_KV_HEADS = 3
KV_W = 384
WINDOW = 128
ROT_DIM = 32
ROPE_THETA = 500000.0
EPS = 1e-6
NEG = -1e30

VMEM_LIMIT_BYTES = 56 * 1024 * 1024
VMEM_LIMIT_BYTES_LARGE = 60 * 1024 * 1024
LANES = 128
MM_ROW_CHUNK = 512
POOL_HALO = 16


def _params(*sem, vmem_limit_bytes=VMEM_LIMIT_BYTES):
    return pltpu.CompilerParams(dimension_semantics=sem, vmem_limit_bytes=vmem_limit_bytes)


def _rmsnorm_kernel(x_ref, g_ref, o_ref):
    x = x_ref[...]
    ms = jnp.mean(x * x, axis=-1, keepdims=True)
    o_ref[...] = (x * lax.rsqrt(ms + EPS) * g_ref[...]).astype(o_ref.dtype)


def rmsnorm(x, g, out_dtype, rows=512):
    r, d = x.shape
    return pl.pallas_call(
        _rmsnorm_kernel,
        out_shape=jax.ShapeDtypeStruct((r, d), out_dtype),
        grid=(r // rows,),
        in_specs=[pl.BlockSpec((rows, d), lambda i: (i, 0)),
                  pl.BlockSpec((1, d), lambda i: (0, 0))],
        out_specs=pl.BlockSpec((rows, d), lambda i: (i, 0)),
        compiler_params=_params("parallel"),
        name="rmsnorm",
    )(x, g.reshape(1, d))


def _epilogue(acc, epilogue, res_ref, o_ref, rs):
    if epilogue == "relu2":
        acc = jnp.square(jnp.maximum(acc, 0.0))
    elif epilogue == "residual":
        acc = res_ref[rs, :] + acc
    o_ref[rs, :] = acc.astype(o_ref.dtype)
    return acc


def _lane_group_sumsq(h_new):
    sq = h_new * h_new
    pss = sq[:, :LANES]
    for c in range(1, sq.shape[1] // LANES):
        pss = pss + sq[:, c * LANES:(c + 1) * LANES]
    return pss


def _row_rstd(pss_ref, d):
    return lax.rsqrt(jnp.sum(pss_ref[...], axis=-1, keepdims=True) * (1.0 / d) + EPS)


def _mm_wres_kernel(*refs, n_a, k_splits, chunk, n_j, row_chunk, epilogue, normed, emit_norm, side):
    refs = list(refs)
    a_refs = [refs.pop(0) for _ in range(n_a)]
    w_ref = refs.pop(0)
    g_ref = refs.pop(0) if normed else None
    pss_in_ref = refs.pop(0) if normed else None
    res_ref = refs.pop(0) if epilogue == "residual" else None
    side_in_ref = refs.pop(0) if side else None
    o_ref = refs.pop(0)
    hb_ref = refs.pop(0) if emit_norm else None
    pss_out_ref = refs.pop(0) if emit_norm else None
    side_out_ref = refs.pop(0) if side else None
    (wbf_ref,) = refs
    jp = pl.program_id(0)
    i = pl.program_id(1)
    n_slots = wbf_ref.shape[0]

    @pl.when(jp < n_j)
    def _():
        row0 = pl.multiple_of(i * chunk, chunk)
        w = w_ref[...]
        if normed:
            w = w * g_ref[...]
        wbf_ref[jp % n_slots, pl.ds(row0, chunk), :] = w.astype(jnp.bfloat16)

    @pl.when(jp > 0)
    def _():
        use_slot = (jp - 1) % n_slots
        rstd = _row_rstd(pss_in_ref, wbf_ref.shape[1]) if normed else None
        for r0 in range(0, o_ref.shape[0], row_chunk):
            rs = slice(r0, r0 + row_chunk)
            acc = None
            off = 0
            for a_ref, kw in zip(a_refs, k_splits):
                part = jnp.dot(a_ref[rs, :], wbf_ref[use_slot, pl.ds(off, kw), :], preferred_element_type=jnp.float32)
                acc = part if acc is None else acc + part
                off += kw
            if normed:
                acc = acc * rstd[rs]
            total = _epilogue(acc, epilogue, res_ref, o_ref, rs)
            if emit_norm:
                hb_ref[rs, :] = total.astype(hb_ref.dtype)
                pss_out_ref[rs, :] = _lane_group_sumsq(total)
        if side:
            side_out_ref[...] = side_in_ref[...].astype(side_out_ref.dtype)


def matmul_wres(a_list, w, layer, n_out, *, epilogue="cast", res=None, out_dtype=jnp.bfloat16, bm=1024, bn=1024,
                norm=None, emit_norm=False, side=None):
    m = a_list[0].shape[0]
    k_total = w.shape[1]
    k_splits = tuple(a.shape[1] for a in a_list)
    n_i, n_j = m // bm, n_out // bn
    chunk = k_total // n_i
    assert sum(k_splits) == k_total and n_i * bm == m and n_j * bn == n_out and chunk * n_i == k_total
    assert chunk % 16 == 0 and (not emit_norm or epilogue == "residual")

    def row_blk(jp, i):
        return jnp.where(jp == 0, 0, i)

    def stage_blk(jp, i):
        return jnp.where(jp < n_j, i, 0)

    o_map = lambda jp, i: (row_blk(jp, i), jnp.maximum(jp - 1, 0))
    in_specs = [pl.BlockSpec((bm, kw), lambda jp, i: (row_blk(jp, i), 0)) for kw in k_splits]
    in_specs.append(pl.BlockSpec((None, chunk, bn), lambda jp, i: (layer, stage_blk(jp, i), jnp.minimum(jp, n_j - 1))))
    args = list(a_list) + [w]
    if norm is not None:
        gain, pss = norm
        in_specs.append(pl.BlockSpec((chunk, 1), lambda jp, i: (stage_blk(jp, i), 0)))
        in_specs.append(pl.BlockSpec((bm, pss.shape[1]), lambda jp, i: (row_blk(jp, i), 0)))
        args += [gain.reshape(k_total, 1), pss]
    if epilogue == "residual":
        in_specs.append(pl.BlockSpec((bm, bn), o_map))
        args.append(res)
    out_shape = [jax.ShapeDtypeStruct((m, n_out), out_dtype)]
    out_specs = [pl.BlockSpec((bm, bn), o_map)]
    if emit_norm:
        out_shape += [jax.ShapeDtypeStruct((m, n_out), jnp.bfloat16),
                      jax.ShapeDtypeStruct((m, n_j * LANES), jnp.float32)]
        out_specs += [pl.BlockSpec((bm, bn), o_map), pl.BlockSpec((bm, LANES), o_map)]
    if side is not None:
        w_side, layer2 = side
        _, k2, n2 = w_side.shape
        rows2 = k2 // (n_j * n_i)
        assert rows2 * n_j * n_i == k2 and rows2 % 16 == 0
        slab = lambda jp, i: jnp.where(jp == 0, 0, (jp - 1) * n_i + i)
        in_specs.append(pl.BlockSpec((None, rows2, n2), lambda jp, i: (layer2, slab(jp, i), 0)))
        args.append(w_side)
        out_shape.append(jax.ShapeDtypeStruct((k2, n2), jnp.bfloat16))
        out_specs.append(pl.BlockSpec((rows2, n2), lambda jp, i: (slab(jp, i), 0)))
    outs = pl.pallas_call(
        functools.partial(_mm_wres_kernel, n_a=len(a_list), k_splits=k_splits, chunk=chunk, n_j=n_j,
                          row_chunk=min(bm, MM_ROW_CHUNK), epilogue=epilogue,
                          normed=norm is not None, emit_norm=emit_norm, side=side is not None),
        out_shape=out_shape,
        grid=(n_j + 1, n_i),
        in_specs=in_specs,
        out_specs=out_specs,
        scratch_shapes=[pltpu.VMEM((min(2, n_j), k_total, bn), jnp.bfloat16)],
        compiler_params=_params("arbitrary", "arbitrary"),
        name="mm_wres_" + epilogue,
    )(*args)
    return outs[0] if len(outs) == 1 else tuple(outs)


def _mm_kacc_kernel(a_ref, w_ref, res_ref, o_ref, *norm_refs, nk):
    k = pl.program_id(2)

    @pl.when(k == 0)
    def _():
        o_ref[...] = res_ref[...]

    def partial_product():
        return jnp.dot(a_ref[...], w_ref[...], preferred_element_type=jnp.float32)

    if not norm_refs:
        o_ref[...] += partial_product()
    else:
        @pl.when(k < nk - 1)
        def _():
            o_ref[...] += partial_product()

        @pl.when(k == nk - 1)
        def _():
            total = o_ref[...] + partial_product()
            o_ref[...] = total
            hb_ref, pss_ref = norm_refs
            hb_ref[...] = total.astype(hb_ref.dtype)
            pss_ref[...] = _lane_group_sumsq(total)


def matmul_kacc(a, w, res, *, emit_norm=False, bm=1024, bn=1024):
    m, k_total = a.shape
    n = w.shape[1]
    bk, vmem_limit = (2048, VMEM_LIMIT_BYTES) if emit_norm else (4096, VMEM_LIMIT_BYTES_LARGE)
    assert m % bm == 0 and n % bn == 0 and k_total % bk == 0
    o_map = lambda j, i, k: (i, j)
    out_shape = [jax.ShapeDtypeStruct((m, n), jnp.float32)]
    out_specs = [pl.BlockSpec((bm, bn), o_map)]
    if emit_norm:
        out_shape += [jax.ShapeDtypeStruct((m, n), jnp.bfloat16),
                      jax.ShapeDtypeStruct((m, (n // bn) * LANES), jnp.float32)]
        out_specs += [pl.BlockSpec((bm, bn), o_map), pl.BlockSpec((bm, LANES), o_map)]
    outs = pl.pallas_call(
        functools.partial(_mm_kacc_kernel, nk=k_total // bk),
        out_shape=out_shape,
        grid=(n // bn, m // bm, k_total // bk),
        in_specs=[pl.BlockSpec((bm, bk), lambda j, i, k: (i, k)),
                  pl.BlockSpec((bk, bn), lambda j, i, k: (k, j)),
                  pl.BlockSpec((bm, bn), o_map)],
        out_specs=out_specs,
        compiler_params=_params("parallel", "parallel", "arbitrary", vmem_limit_bytes=vmem_limit),
        name="mm_kacc_residual",
    )(a, w, res)
    return outs[0] if len(outs) == 1 else tuple(outs)


def _mm_rows_kernel(a_ref, w_ref, o_ref):
    o_ref[...] = jnp.dot(a_ref[...], w_ref[...].astype(jnp.bfloat16),
                         preferred_element_type=jnp.float32).astype(o_ref.dtype)


def matmul_rows(a, w, layer, *, bn=512):
    m, k_total = a.shape
    n = w.shape[2]
    assert n % bn == 0
    return pl.pallas_call(
        _mm_rows_kernel,
        out_shape=jax.ShapeDtypeStruct((m, n), jnp.bfloat16),
        grid=(n // bn,),
        in_specs=[pl.BlockSpec((m, k_total), lambda j: (0, 0)),
                  pl.BlockSpec((None, k_total, bn), lambda j: (layer, 0, j))],
        out_specs=pl.BlockSpec((m, bn), lambda j: (0, j)),
        compiler_params=_params("parallel"),
        name="mm_rows",
    )(a, w)


def _pool_kernel(u_ref, halo_ref, w_ref, scale_ref, o_ref, *, ts):
    i = pl.program_id(1)
    cur = u_ref[0].astype(jnp.float32)
    halo = halo_ref[0].astype(jnp.float32)
    halo = jnp.where(i > 0, halo, 0.0)
    t1 = (i * ts + 1 + lax.broadcasted_iota(jnp.int32, (ts, 1), 0)).astype(jnp.float32)
    for g, win in enumerate(POOL_WINDOWS):
        c0, c1 = g * POOL_GROUP, (g + 1) * POOL_GROUP
        x = jnp.concatenate([halo[:, c0:c1], cur[:, c0:c1]], axis=0)
        s = x
        d = 1
        while d < win:
            s = s[d:] + s[:-d]
            d *= 2
        s = s[POOL_HALO - (win - 1):]
        mean = s / jnp.minimum(t1, float(win))
        p = (mean - cur[:, c0:c1]).astype(jnp.bfloat16)
        y = jnp.dot(p, w_ref[g], preferred_element_type=jnp.float32)
        o_ref[0, :, c0:c1] = (y * scale_ref[:, c0:c1]).astype(o_ref.dtype)


def pool_mixer(proj, w_group, scale, ts=512):
    b, s, _ = proj.shape
    hb = ts // POOL_HALO
    return pl.pallas_call(
        functools.partial(_pool_kernel, ts=ts),
        out_shape=jax.ShapeDtypeStruct((b, s, SELF_W), jnp.bfloat16),
        grid=(b, s // ts),
        in_specs=[
            pl.BlockSpec((1, ts, SELF_W), lambda bi, i: (bi, i, 0)),
            pl.BlockSpec((1, POOL_HALO, SELF_W), lambda bi, i: (bi, jnp.maximum(i * hb - 1, 0), 0)),
            pl.BlockSpec((len(POOL_WINDOWS), POOL_GROUP, POOL_GROUP), lambda bi, i: (0, 0, 0)),
            pl.BlockSpec((1, SELF_W), lambda bi, i: (0, 0)),
        ],
        out_specs=pl.BlockSpec((1, ts, SELF_W), lambda bi, i: (bi, i, 0)),
        compiler_params=_params("parallel", "parallel"),
        name="pool_mixer",
    )(proj, proj, w_group, scale.reshape(1, SELF_W))


def _xattn_kernel(*refs):
    q_refs, kv_ref, o_ref = refs[:XA_HEADS], refs[XA_HEADS], refs[XA_HEADS + 1]
    for h in range(XA_HEADS):
        hs = slice(h * XA_HEAD_DIM, (h + 1) * XA_HEAD_DIM)
        q = q_refs[h][0]
        k = kv_ref[0, :, hs]
        v = kv_ref[0, :, XA_W + h * XA_HEAD_DIM:XA_W + (h + 1) * XA_HEAD_DIM]
        s = lax.dot_general(q, k, (((1,), (1,)), ((), ())), preferred_element_type=jnp.float32)
        s = s * (XA_HEAD_DIM ** -0.5)
        m = jnp.max(s, axis=-1, keepdims=True)
        e = jnp.exp(s - m)
        l = jnp.sum(e, axis=-1, keepdims=True)
        o = jnp.dot(e.astype(jnp.bfloat16), v, preferred_element_type=jnp.float32)
        o_ref[0, :, hs] = (o / l).astype(o_ref.dtype)


def mem_xattn(proj, q_col0, mem_kv, ts=1024):
    b, s, _ = proj.shape
    qb = q_col0 // XA_HEAD_DIM
    assert qb * XA_HEAD_DIM == q_col0
    q_specs = [pl.BlockSpec((1, ts, XA_HEAD_DIM), functools.partial(lambda bi, i, h: (bi, i, qb + h), h=h))
               for h in range(XA_HEADS)]
    return pl.pallas_call(
        _xattn_kernel,
        out_shape=jax.ShapeDtypeStruct((b, s, XA_W), jnp.bfloat16),
        grid=(b, s // ts),
        in_specs=q_specs + [pl.BlockSpec((1, MEM_LEN, 2 * XA_W), lambda bi, i: (bi, 0, 0))],
        out_specs=pl.BlockSpec((1, ts, XA_W), lambda bi, i: (bi, i, 0)),
        compiler_params=_params("parallel", "parallel"),
        name="mem_xattn",
    )(*([proj] * XA_HEADS), mem_kv)


def _rope_table_kernel(pos_ref, freq_ref, sign_ref, cos_ref, sin_ref):
    ang = pos_ref[0].astype(jnp.float32) * freq_ref[...]
    cos_ref[0] = jnp.cos(ang)
    sin_ref[0] = jnp.sin(ang) * sign_ref[...]


def rope_tables(positions, ts=1024):
    b, s = positions.shape
    half = ROT_DIM // 2
    inv_freq = ROPE_THETA ** (-jnp.arange(0, ROT_DIM, 2, dtype=jnp.float32) / ROT_DIM)
    freq = jnp.concatenate([inv_freq, inv_freq, jnp.zeros((HEAD_DIM - ROT_DIM,), jnp.float32)]).reshape(1, HEAD_DIM)
    sign = jnp.concatenate([-jnp.ones((half,), jnp.float32), jnp.ones((half,), jnp.float32),
                            jnp.zeros((HEAD_DIM - ROT_DIM,), jnp.float32)]).reshape(1, HEAD_DIM)
    out = jax.ShapeDtypeStruct((b, s, HEAD_DIM), jnp.float32)
    return pl.pallas_call(
        _rope_table_kernel,
        out_shape=(out, out),
        grid=(b, s // ts),
        in_specs=[pl.BlockSpec((1, ts, 1), lambda bi, i: (bi, i, 0)),
                  pl.BlockSpec((1, HEAD_DIM), lambda bi, i: (0, 0)),
                  pl.BlockSpec((1, HEAD_DIM), lambda bi, i: (0, 0))],
        out_specs=(pl.BlockSpec((1, ts, HEAD_DIM), lambda bi, i: (bi, i, 0)),
                   pl.BlockSpec((1, ts, HEAD_DIM), lambda bi, i: (bi, i, 0))),
        compiler_params=_params("parallel", "parallel"),
        name="rope_tables",
    )(positions.reshape(b, s, 1), freq, sign)


def _rope(xb, cos_t, sin_t, rot):
    partner = jnp.dot(xb, rot, preferred_element_type=jnp.float32)
    return xb.astype(jnp.float32) * cos_t + partner * sin_t


def _swa_kernel(sink_ref, q_ref, kc_ref, vc_ref, kp_ref, vp_ref, cosc_ref, sinc_ref, cosp_ref, sinp_ref, rot_ref,
                o_ref):
    n = pl.program_id(1)
    blk = WINDOW
    cols = GQA_GROUP * blk
    cos_c, sin_c = cosc_ref[0], sinc_ref[0]
    cos_p, sin_p = cosp_ref[0], sinp_ref[0]
    rot = rot_ref[...]

    kj = lax.broadcasted_iota(jnp.int32, (blk, cols), 0)
    col = lax.broadcasted_iota(jnp.int32, (blk, cols), 1)
    cur_live = kj <= col % blk
    live = jnp.logical_or(cur_live, n > 0)
    col_group = lax.broadcasted_iota(jnp.int32, (1, cols), 1) // blk

    for h in range(N_KV_HEADS):
        hs = slice(h * HEAD_DIM, (h + 1) * HEAD_DIM)
        kp = _rope(kp_ref[0, :, hs], cos_p, sin_p, rot)
        kc = _rope(kc_ref[0, :, hs], cos_c, sin_c, rot)
        k = jnp.concatenate([kp, kc], axis=0).astype(jnp.bfloat16)
        v = jnp.concatenate([vp_ref[0, :, hs], vc_ref[0, :, hs]], axis=0)
        q_parts = []
        sink = jnp.zeros((1, cols), jnp.float32)
        for g in range(GQA_GROUP):
            c0 = (h * GQA_GROUP + g) * HEAD_DIM
            q_parts.append(_rope(q_ref[0, :, c0:c0 + HEAD_DIM], cos_c, sin_c, rot))
            sink = jnp.where(col_group == g, sink_ref[h * GQA_GROUP + g], sink)
        q = jnp.concatenate(q_parts, axis=0).astype(jnp.bfloat16)
        st = lax.dot_general(k, q, (((1,), (1,)), ((), ())), preferred_element_type=jnp.float32)
        s = jnp.where(cur_live, st[blk:], st[:blk]) * (HEAD_DIM ** -0.5)
        s = jnp.where(live, s, NEG)
        m = jnp.maximum(jnp.max(s, axis=0, keepdims=True), sink)
        e = jnp.exp(s - m)
        l = jnp.sum(e, axis=0, keepdims=True) + jnp.exp(sink - m)
        p = e * (1.0 / l)
        pt = jnp.concatenate([jnp.where(cur_live, 0.0, p), jnp.where(cur_live, p, 0.0)], axis=0)
        o = lax.dot_general(pt.astype(jnp.bfloat16), v, (((0,), (0,)), ((), ())),
                            preferred_element_type=jnp.float32)
        for g in range(GQA_GROUP):
            c0 = (h * GQA_GROUP + g) * HEAD_DIM
            o_ref[0, :, c0:c0 + HEAD_DIM] = o[g * blk:(g + 1) * blk].astype(o_ref.dtype)


def swa_attention(q, kvx, sink, cos_t, sin_t):
    b, s, _ = q.shape
    blk = WINDOW
    half = ROT_DIM // 2
    src_lane = lax.broadcasted_iota(jnp.int32, (HEAD_DIM, HEAD_DIM), 0)
    dst_lane = lax.broadcasted_iota(jnp.int32, (HEAD_DIM, HEAD_DIM), 1)
    partner_of_dst = jnp.where(dst_lane < half, dst_lane + half, dst_lane - half)
    rot = jnp.logical_and(dst_lane < ROT_DIM, src_lane == partner_of_dst).astype(jnp.bfloat16)
    cur = lambda bi, n, sk: (bi, n, 0)
    prev = lambda bi, n, sk: (bi, jnp.maximum(n - 1, 0), 0)
    grid_spec = pltpu.PrefetchScalarGridSpec(
        num_scalar_prefetch=1,
        grid=(b, s // blk),
        in_specs=[
            pl.BlockSpec((1, blk, SELF_W), cur),
            pl.BlockSpec((1, blk, KV_W), cur),
            pl.BlockSpec((1, blk, KV_W), lambda bi, n, sk: (bi, n, 1)),
            pl.BlockSpec((1, blk, KV_W), prev),
            pl.BlockSpec((1, blk, KV_W), lambda bi, n, sk: (bi, jnp.maximum(n - 1, 0), 1)),
            pl.BlockSpec((1, blk, HEAD_DIM), cur),
            pl.BlockSpec((1, blk, HEAD_DIM), cur),
            pl.BlockSpec((1, blk, HEAD_DIM), prev),
            pl.BlockSpec((1, blk, HEAD_DIM), prev),
            pl.BlockSpec((HEAD_DIM, HEAD_DIM), lambda bi, n, sk: (0, 0)),
        ],
        out_specs=pl.BlockSpec((1, blk, SELF_W), cur),
    )
    return pl.pallas_call(
        _swa_kernel,
        out_shape=jax.ShapeDtypeStruct((b, s, SELF_W), jnp.bfloat16),
        grid_spec=grid_spec,
        compiler_params=_params("parallel", "parallel"),
        name="swa_attention",
    )(sink, q, kvx, kvx, kvx, kvx, cos_t, sin_t, cos_t, sin_t, rot)


def kernel(x, mem, positions, norm_mix, norm_mem, norm_mlp, w_mem_kv, pool_w_in, pool_w_group, pool_scale,
           pool_w_out, attn_w_in, attn_sink, attn_w_out, mlp_w1, mlp_w2, final_norm):
    b, s, d = x.shape
    t = b * s
    bf16 = jnp.bfloat16
    h = x.reshape(t, d)
    mem2 = mem.reshape(b * MEM_LEN, d)
    cos_t, sin_t = rope_tables(positions)

    hb = pss = None
    for i in range(2):
        memn = rmsnorm(mem2, norm_mem[i], bf16)
        mem_kv = matmul_rows(memn, w_mem_kv, i).reshape(b, MEM_LEN, 2 * XA_W)
        if i == 0:
            hn = rmsnorm(h, norm_mix[0], bf16)
            proj = matmul_wres([hn], pool_w_in, 0, SELF_W + XA_W).reshape(b, s, -1)
            y_self = pool_mixer(proj, pool_w_group[0].astype(bf16), pool_scale[0])
            y_mem = mem_xattn(proj, SELF_W, mem_kv)
            w_out = pool_w_out
        else:
            norm = (norm_mix[i], pss)
            q = matmul_wres([hb], attn_w_in, 0, SELF_W, norm=norm).reshape(b, s, -1)
            kvx = matmul_wres([hb], attn_w_in[:, :, SELF_W:], 0, 2 * KV_W + XA_W, bn=2 * KV_W + XA_W, norm=norm)
            kvx = kvx.reshape(b, s, -1)
            y_self = swa_attention(q, kvx, attn_sink[0], cos_t, sin_t)
            y_mem = mem_xattn(kvx, 2 * KV_W, mem_kv)
            w_out = attn_w_out
        h, hb, pss = matmul_wres([y_self.reshape(t, SELF_W), y_mem.reshape(t, XA_W)], w_out, 0, d,
                                 epilogue="residual", res=h, out_dtype=jnp.float32, bm=512, emit_norm=True)
        a, w2b = matmul_wres([hb], mlp_w1, i, mlp_w1.shape[2], epilogue="relu2", norm=(norm_mlp[i], pss),
                             side=(mlp_w2, i))
        if i == 0:
            h, hb, pss = matmul_kacc(a, w2b, h, emit_norm=True)
        else:
            h = matmul_kacc(a, w2b, h)
    return rmsnorm(h, final_norm, jnp.float32).reshape(b, s, d)
```

```python
import functools

import jax
import jax.numpy as jnp
from jax import lax
from jax.experimental import pallas as pl
from jax.experimental.pallas import tpu as pltpu

D_MODEL = 4096
SELF_W = 3072
XA_W = 1024
XA_HEADS = 4
XA_HEAD_DIM = 256
MEM_LEN = 256
POOL_WINDOWS = (2, 4, 8, 16)
POOL_GROUP = 768
HEAD_DIM = 128
N_Q_HEADS = 24
GQA_GROUP = 8
N_KV_HEADS = 3
KV_W = 384
WINDOW = 128
ROT_DIM = 32
ROPE_THETA = 500000.0
EPS = 1e-6
NEG = -1e30

VMEM_LIMIT_BYTES = 56 * 1024 * 1024
VMEM_LIMIT_BYTES_LARGE = 60 * 1024 * 1024
LANES = 128
MM_ROW_CHUNK = 512
SWA_BLOCKS = 4
POOL_HALO = 16


def _params(*sem, vmem_limit_bytes=VMEM_LIMIT_BYTES):
    return pltpu.CompilerParams(dimension_semantics=sem, vmem_limit_bytes=vmem_limit_bytes)


def _rmsnorm_kernel(x_ref, g_ref, o_ref):
    x = x_ref[...]
    ms = jnp.mean(x * x, axis=-1, keepdims=True)
    o_ref[...] = (x * lax.rsqrt(ms + EPS) * g_ref[...]).astype(o_ref.dtype)


def rmsnorm(x, g, out_dtype, rows=512):
    r, d = x.shape
    return pl.pallas_call(
        _rmsnorm_kernel,
        out_shape=jax.ShapeDtypeStruct((r, d), out_dtype),
        grid=(r // rows,),
        in_specs=[pl.BlockSpec((rows, d), lambda i: (i, 0)),
                  pl.BlockSpec((1, d), lambda i: (0, 0))],
        out_specs=pl.BlockSpec((rows, d), lambda i: (i, 0)),
        compiler_params=_params("parallel"),
        name="rmsnorm",
    )(x, g.reshape(1, d))


def _epilogue(acc, epilogue, res_ref, o_ref, rs):
    if epilogue == "relu2":
        acc = jnp.square(jnp.maximum(acc, 0.0))
    elif epilogue == "residual":
        acc = res_ref[rs, :] + acc
    o_ref[rs, :] = acc.astype(o_ref.dtype)
    return acc


def _lane_group_sumsq(h_new):
    sq = h_new * h_new
    pss = sq[:, :LANES]
    for c in range(1, sq.shape[1] // LANES):
        pss = pss + sq[:, c * LANES:(c + 1) * LANES]
    return pss


def _row_rstd(pss_ref, d):
    return lax.rsqrt(jnp.sum(pss_ref[...], axis=-1, keepdims=True) * (1.0 / d) + EPS)


def _mm_wres_kernel(*refs, n_a, k_splits, chunk, n_j, row_chunk, epilogue, normed, emit_norm, side):
    refs = list(refs)
    a_refs = [refs.pop(0) for _ in range(n_a)]
    w_ref = refs.pop(0)
    g_ref = refs.pop(0) if normed else None
    pss_in_ref = refs.pop(0) if normed else None
    res_ref = refs.pop(0) if epilogue == "residual" else None
    side_in_ref = refs.pop(0) if side else None
    o_ref = refs.pop(0)
    hb_ref = refs.pop(0) if emit_norm else None
    pss_out_ref = refs.pop(0) if emit_norm else None
    side_out_ref = refs.pop(0) if side else None
    (wbf_ref,) = refs
    jp = pl.program_id(0)
    i = pl.program_id(1)
    n_slots = wbf_ref.shape[0]

    @pl.when(jp < n_j)
    def _():
        row0 = pl.multiple_of(i * chunk, chunk)
        w = w_ref[...]
        if normed:
            w = w * g_ref[...]
        wbf_ref[jp % n_slots, pl.ds(row0, chunk), :] = w.astype(jnp.bfloat16)

    @pl.when(jp > 0)
    def _():
        use_slot = (jp - 1) % n_slots
        rstd = _row_rstd(pss_in_ref, wbf_ref.shape[1]) if normed else None
        for r0 in range(0, o_ref.shape[0], row_chunk):
            rs = slice(r0, r0 + row_chunk)
            acc = None
            off = 0
            for a_ref, kw in zip(a_refs, k_splits):
                part = jnp.dot(a_ref[rs, :], wbf_ref[use_slot, pl.ds(off, kw), :], preferred_element_type=jnp.float32)
                acc = part if acc is None else acc + part
                off += kw
            if normed:
                acc = acc * rstd[rs]
            total = _epilogue(acc, epilogue, res_ref, o_ref, rs)
            if emit_norm:
                hb_ref[rs, :] = total.astype(hb_ref.dtype)
                pss_out_ref[rs, :] = _lane_group_sumsq(total)
        if side:
            side_out_ref[...] = side_in_ref[...].astype(side_out_ref.dtype)


def matmul_wres(a_list, w, layer, n_out, *, epilogue="cast", res=None, out_dtype=jnp.bfloat16, bm=1024, bn=1024,
                norm=None, emit_norm=False, side=None, vmem_limit_bytes=VMEM_LIMIT_BYTES):
    m = a_list[0].shape[0]
    k_total = w.shape[1]
    k_splits = tuple(a.shape[1] for a in a_list)
    n_i, n_j = m // bm, n_out // bn
    chunk = k_total // n_i
    assert sum(k_splits) == k_total and n_i * bm == m and n_j * bn == n_out and chunk * n_i == k_total
    assert chunk % 16 == 0 and (not emit_norm or epilogue == "residual")

    def row_blk(jp, i):
        return jnp.where(jp == 0, 0, i)

    def stage_blk(jp, i):
        return jnp.where(jp < n_j, i, 0)

    o_map = lambda jp, i: (row_blk(jp, i), jnp.maximum(jp - 1, 0))
    in_specs = [pl.BlockSpec((bm, kw), lambda jp, i: (row_blk(jp, i), 0)) for kw in k_splits]
    in_specs.append(pl.BlockSpec((None, chunk, bn), lambda jp, i: (layer, stage_blk(jp, i), jnp.minimum(jp, n_j - 1))))
    args = list(a_list) + [w]
    if norm is not None:
        gain, pss = norm
        in_specs.append(pl.BlockSpec((chunk, 1), lambda jp, i: (stage_blk(jp, i), 0)))
        in_specs.append(pl.BlockSpec((bm, pss.shape[1]), lambda jp, i: (row_blk(jp, i), 0)))
        args += [gain.reshape(k_total, 1), pss]
    if epilogue == "residual":
        in_specs.append(pl.BlockSpec((bm, bn), o_map))
        args.append(res)
    out_shape = [jax.ShapeDtypeStruct((m, n_out), out_dtype)]
    out_specs = [pl.BlockSpec((bm, bn), o_map)]
    if emit_norm:
        out_shape += [jax.ShapeDtypeStruct((m, n_out), jnp.bfloat16),
                      jax.ShapeDtypeStruct((m, n_j * LANES), jnp.float32)]
        out_specs += [pl.BlockSpec((bm, bn), o_map), pl.BlockSpec((bm, LANES), o_map)]
    if side is not None:
        w_side, layer2 = side
        _, k2, n2 = w_side.shape
        rows2 = k2 // (n_j * n_i)
        assert rows2 * n_j * n_i == k2 and rows2 % 16 == 0
        slab = lambda jp, i: jnp.where(jp == 0, 0, (jp - 1) * n_i + i)
        in_specs.append(pl.BlockSpec((None, rows2, n2), lambda jp, i: (layer2, slab(jp, i), 0)))
        args.append(w_side)
        out_shape.append(jax.ShapeDtypeStruct((k2, n2), jnp.bfloat16))
        out_specs.append(pl.BlockSpec((rows2, n2), lambda jp, i: (slab(jp, i), 0)))
    outs = pl.pallas_call(
        functools.partial(_mm_wres_kernel, n_a=len(a_list), k_splits=k_splits, chunk=chunk, n_j=n_j,
                          row_chunk=min(bm, MM_ROW_CHUNK), epilogue=epilogue,
                          normed=norm is not None, emit_norm=emit_norm, side=side is not None),
        out_shape=out_shape,
        grid=(n_j + 1, n_i),
        in_specs=in_specs,
        out_specs=out_specs,
        scratch_shapes=[pltpu.VMEM((min(2, n_j), k_total, bn), jnp.bfloat16)],
        compiler_params=_params("arbitrary", "arbitrary", vmem_limit_bytes=vmem_limit_bytes),
        name="mm_wres_" + epilogue,
    )(*args)
    return outs[0] if len(outs) == 1 else tuple(outs)


def _mm_kacc_kernel(a_ref, w_ref, res_ref, o_ref, *norm_refs, nk):
    k = pl.program_id(2)

    @pl.when(k == 0)
    def _():
        o_ref[...] = res_ref[...]

    row_slices = [slice(r0, r0 + MM_ROW_CHUNK) for r0 in range(0, o_ref.shape[0], MM_ROW_CHUNK)]

    def partial_product(rs):
        return jnp.dot(a_ref[rs, :], w_ref[...], preferred_element_type=jnp.float32)

    def accumulate():
        for rs in row_slices:
            o_ref[rs, :] += partial_product(rs)

    if not norm_refs:
        accumulate()
    else:
        pl.when(k < nk - 1)(accumulate)

        @pl.when(k == nk - 1)
        def _():
            hb_ref, pss_ref = norm_refs
            for rs in row_slices:
                total = o_ref[rs, :] + partial_product(rs)
                o_ref[rs, :] = total
                hb_ref[rs, :] = total.astype(hb_ref.dtype)
                pss_ref[rs, :] = _lane_group_sumsq(total)


def matmul_kacc(a, w, res, *, emit_norm=False, bm=1024, bn=1024):
    m, k_total = a.shape
    n = w.shape[1]
    bk, vmem_limit = (4096, VMEM_LIMIT_BYTES_LARGE) if emit_norm else (4096, VMEM_LIMIT_BYTES_LARGE)
    assert m % bm == 0 and n % bn == 0 and k_total % bk == 0
    o_map = lambda j, i, k: (i, j)
    out_shape = [jax.ShapeDtypeStruct((m, n), jnp.float32)]
    out_specs = [pl.BlockSpec((bm, bn), o_map)]
    if emit_norm:
        out_shape += [jax.ShapeDtypeStruct((m, n), jnp.bfloat16),
                      jax.ShapeDtypeStruct((m, (n // bn) * LANES), jnp.float32)]
        out_specs += [pl.BlockSpec((bm, bn), o_map), pl.BlockSpec((bm, LANES), o_map)]
    outs = pl.pallas_call(
        functools.partial(_mm_kacc_kernel, nk=k_total // bk),
        out_shape=out_shape,
        grid=(n // bn, m // bm, k_total // bk),
        in_specs=[pl.BlockSpec((bm, bk), lambda j, i, k: (i, k)),
                  pl.BlockSpec((bk, bn), lambda j, i, k: (k, j)),
                  pl.BlockSpec((bm, bn), o_map)],
        out_specs=out_specs,
        compiler_params=_params("parallel", "parallel", "arbitrary", vmem_limit_bytes=vmem_limit),
        name="mm_kacc_residual",
    )(a, w, res)
    return outs[0] if len(outs) == 1 else tuple(outs)


def _mm_rows_kernel(a_ref, w_ref, o_ref):
    o_ref[...] = jnp.dot(a_ref[...], w_ref[...].astype(jnp.bfloat16),
                         preferred_element_type=jnp.float32).astype(o_ref.dtype)


def matmul_rows(a, w, layer, *, bn=512):
    m, k_total = a.shape
    n = w.shape[2]
    assert n % bn == 0
    return pl.pallas_call(
        _mm_rows_kernel,
        out_shape=jax.ShapeDtypeStruct((m, n), jnp.bfloat16),
        grid=(n // bn,),
        in_specs=[pl.BlockSpec((m, k_total), lambda j: (0, 0)),
                  pl.BlockSpec((None, k_total, bn), lambda j: (layer, 0, j))],
        out_specs=pl.BlockSpec((m, bn), lambda j: (0, j)),
        compiler_params=_params("parallel"),
        name="mm_rows",
    )(a, w)


def _pool_kernel(u_ref, halo_ref, w_ref, scale_ref, o_ref, *, ts):
    i = pl.program_id(1)
    cur = u_ref[0].astype(jnp.float32)
    halo = halo_ref[0].astype(jnp.float32)
    halo = jnp.where(i > 0, halo, 0.0)
    t1 = (i * ts + 1 + lax.broadcasted_iota(jnp.int32, (ts, 1), 0)).astype(jnp.float32)
    for g, win in enumerate(POOL_WINDOWS):
        c0, c1 = g * POOL_GROUP, (g + 1) * POOL_GROUP
        x = jnp.concatenate([halo[:, c0:c1], cur[:, c0:c1]], axis=0)
        s = x
        d = 1
        while d < win:
            s = s[d:] + s[:-d]
            d *= 2
        s = s[POOL_HALO - (win - 1):]
        mean = s / jnp.minimum(t1, float(win))
        p = (mean - cur[:, c0:c1]).astype(jnp.bfloat16)
        y = jnp.dot(p, w_ref[g], preferred_element_type=jnp.float32)
        o_ref[0, :, c0:c1] = (y * scale_ref[:, c0:c1]).astype(o_ref.dtype)


def pool_mixer(proj, w_group, scale, ts=512):
    b, s, _ = proj.shape
    hb = ts // POOL_HALO
    return pl.pallas_call(
        functools.partial(_pool_kernel, ts=ts),
        out_shape=jax.ShapeDtypeStruct((b, s, SELF_W), jnp.bfloat16),
        grid=(b, s // ts),
        in_specs=[
            pl.BlockSpec((1, ts, SELF_W), lambda bi, i: (bi, i, 0)),
            pl.BlockSpec((1, POOL_HALO, SELF_W), lambda bi, i: (bi, jnp.maximum(i * hb - 1, 0), 0)),
            pl.BlockSpec((len(POOL_WINDOWS), POOL_GROUP, POOL_GROUP), lambda bi, i: (0, 0, 0)),
            pl.BlockSpec((1, SELF_W), lambda bi, i: (0, 0)),
        ],
        out_specs=pl.BlockSpec((1, ts, SELF_W), lambda bi, i: (bi, i, 0)),
        compiler_params=_params("parallel", "parallel"),
        name="pool_mixer",
    )(proj, proj, w_group, scale.reshape(1, SELF_W))


def _xattn_kernel(*refs):
    q_refs, kv_ref, o_ref = refs[:XA_HEADS], refs[XA_HEADS], refs[XA_HEADS + 1]
    for h in range(XA_HEADS):
        hs = slice(h * XA_HEAD_DIM, (h + 1) * XA_HEAD_DIM)
        q = q_refs[h][0]
        k = kv_ref[0, :, hs]
        v = kv_ref[0, :, XA_W + h * XA_HEAD_DIM:XA_W + (h + 1) * XA_HEAD_DIM]
        s = lax.dot_general(q, k, (((1,), (1,)), ((), ())), preferred_element_type=jnp.float32)
        s = s * (XA_HEAD_DIM ** -0.5)
        m = jnp.max(s, axis=-1, keepdims=True)
        e = jnp.exp(s - m)
        l = jnp.sum(e, axis=-1, keepdims=True)
        o = jnp.dot(e.astype(jnp.bfloat16), v, preferred_element_type=jnp.float32)
        o_ref[0, :, hs] = (o / l).astype(o_ref.dtype)


def mem_xattn(proj, q_col0, mem_kv, ts=1024):
    b, s, _ = proj.shape
    qb = q_col0 // XA_HEAD_DIM
    assert qb * XA_HEAD_DIM == q_col0
    q_specs = [pl.BlockSpec((1, ts, XA_HEAD_DIM), functools.partial(lambda bi, i, h: (bi, i, qb + h), h=h))
               for h in range(XA_HEADS)]
    return pl.pallas_call(
        _xattn_kernel,
        out_shape=jax.ShapeDtypeStruct((b, s, XA_W), jnp.bfloat16),
        grid=(b, s // ts),
        in_specs=q_specs + [pl.BlockSpec((1, MEM_LEN, 2 * XA_W), lambda bi, i: (bi, 0, 0))],
        out_specs=pl.BlockSpec((1, ts, XA_W), lambda bi, i: (bi, i, 0)),
        compiler_params=_params("parallel", "parallel"),
        name="mem_xattn",
    )(*([proj] * XA_HEADS), mem_kv)


def _rope_table_kernel(pos_ref, freq_ref, sign_ref, cos_ref, sin_ref):
    ang = pos_ref[0].astype(jnp.float32) * freq_ref[...]
    cos_ref[0] = jnp.cos(ang)
    sin_ref[0] = jnp.sin(ang) * sign_ref[...]


def rope_tables(positions, ts=1024):
    b, s = positions.shape
    half = ROT_DIM // 2
    inv_freq = ROPE_THETA ** (-jnp.arange(0, ROT_DIM, 2, dtype=jnp.float32) / ROT_DIM)
    freq = jnp.concatenate([inv_freq, inv_freq, jnp.zeros((HEAD_DIM - ROT_DIM,), jnp.float32)]).reshape(1, HEAD_DIM)
    sign = jnp.concatenate([-jnp.ones((half,), jnp.float32), jnp.ones((half,), jnp.float32),
                            jnp.zeros((HEAD_DIM - ROT_DIM,), jnp.float32)]).reshape(1, HEAD_DIM)
    out = jax.ShapeDtypeStruct((b, s, HEAD_DIM), jnp.float32)
    return pl.pallas_call(
        _rope_table_kernel,
        out_shape=(out, out),
        grid=(b, s // ts),
        in_specs=[pl.BlockSpec((1, ts, 1), lambda bi, i: (bi, i, 0)),
                  pl.BlockSpec((1, HEAD_DIM), lambda bi, i: (0, 0)),
                  pl.BlockSpec((1, HEAD_DIM), lambda bi, i: (0, 0))],
        out_specs=(pl.BlockSpec((1, ts, HEAD_DIM), lambda bi, i: (bi, i, 0)),
                   pl.BlockSpec((1, ts, HEAD_DIM), lambda bi, i: (bi, i, 0))),
        compiler_params=_params("parallel", "parallel"),
        name="rope_tables",
    )(positions.reshape(b, s, 1), freq, sign)


def _rope(xb, cos_t, sin_t, rot):
    partner = jnp.dot(xb, rot, preferred_element_type=jnp.float32)
    return xb.astype(jnp.float32) * cos_t + partner * sin_t


def _swa_kernel(sink_ref, q_ref, kc_ref, vc_ref, kp_ref, vp_ref, cosc_ref, sinc_ref, cosp_ref, sinp_ref, rot_ref,
                o_ref):
    m_step = pl.program_id(1)
    blk = WINDOW
    cols = GQA_GROUP * blk
    rot = rot_ref[...]
    cos_b = [cosp_ref[0]] + [cosc_ref[0, t * blk:(t + 1) * blk] for t in range(SWA_BLOCKS)]
    sin_b = [sinp_ref[0]] + [sinc_ref[0, t * blk:(t + 1) * blk] for t in range(SWA_BLOCKS)]

    kj = lax.broadcasted_iota(jnp.int32, (blk, cols), 0)
    col = lax.broadcasted_iota(jnp.int32, (blk, cols), 1)
    cur_live = kj <= col % blk
    first_live = jnp.logical_or(cur_live, m_step > 0)
    col_group = lax.broadcasted_iota(jnp.int32, (1, cols), 1) // blk

    for h in range(N_KV_HEADS):
        hs = slice(h * HEAD_DIM, (h + 1) * HEAD_DIM)
        k_b = [_rope(kp_ref[0, :, hs], cos_b[0], sin_b[0], rot).astype(jnp.bfloat16)]
        v_b = [vp_ref[0, :, hs]]
        for t in range(SWA_BLOCKS):
            rows = slice(t * blk, (t + 1) * blk)
            k_b.append(_rope(kc_ref[0, rows, hs], cos_b[t + 1], sin_b[t + 1], rot).astype(jnp.bfloat16))
            v_b.append(vc_ref[0, rows, hs])
        sink = jnp.zeros((1, cols), jnp.float32)
        for g in range(GQA_GROUP):
            sink = jnp.where(col_group == g, sink_ref[h * GQA_GROUP + g], sink)
        for t in range(SWA_BLOCKS):
            rows = slice(t * blk, (t + 1) * blk)
            k = jnp.concatenate([k_b[t], k_b[t + 1]], axis=0)
            v = jnp.concatenate([v_b[t], v_b[t + 1]], axis=0)
            q_parts = []
            for g in range(GQA_GROUP):
                c0 = (h * GQA_GROUP + g) * HEAD_DIM
                q_parts.append(_rope(q_ref[0, rows, c0:c0 + HEAD_DIM], cos_b[t + 1], sin_b[t + 1], rot))
            q = jnp.concatenate(q_parts, axis=0).astype(jnp.bfloat16)
            st = lax.dot_general(k, q, (((1,), (1,)), ((), ())), preferred_element_type=jnp.float32)
            s = jnp.where(cur_live, st[blk:], st[:blk]) * (HEAD_DIM ** -0.5)
            if t == 0:
                s = jnp.where(first_live, s, NEG)
            m = jnp.maximum(jnp.max(s, axis=0, keepdims=True), sink)
            e = jnp.exp(s - m)
            l = jnp.sum(e, axis=0, keepdims=True) + jnp.exp(sink - m)
            p = e * (1.0 / l)
            pt = jnp.concatenate([jnp.where(cur_live, 0.0, p), jnp.where(cur_live, p, 0.0)], axis=0)
            o = lax.dot_general(pt.astype(jnp.bfloat16), v, (((0,), (0,)), ((), ())),
                                preferred_element_type=jnp.float32)
            for g in range(GQA_GROUP):
                c0 = (h * GQA_GROUP + g) * HEAD_DIM
                o_ref[0, rows, c0:c0 + HEAD_DIM] = o[g * blk:(g + 1) * blk].astype(o_ref.dtype)


def swa_attention(q, kvx, sink, cos_t, sin_t):
    b, s, _ = q.shape
    blk = WINDOW
    tile = SWA_BLOCKS * blk
    half = ROT_DIM // 2
    src_lane = lax.broadcasted_iota(jnp.int32, (HEAD_DIM, HEAD_DIM), 0)
    dst_lane = lax.broadcasted_iota(jnp.int32, (HEAD_DIM, HEAD_DIM), 1)
    partner_of_dst = jnp.where(dst_lane < half, dst_lane + half, dst_lane - half)
    rot = jnp.logical_and(dst_lane < ROT_DIM, src_lane == partner_of_dst).astype(jnp.bfloat16)
    cur = lambda bi, m, sk: (bi, m, 0)
    prev = lambda bi, m, sk: (bi, jnp.maximum(m * SWA_BLOCKS - 1, 0), 0)
    grid_spec = pltpu.PrefetchScalarGridSpec(
        num_scalar_prefetch=1,
        grid=(b, s // tile),
        in_specs=[
            pl.BlockSpec((1, tile, SELF_W), cur),
            pl.BlockSpec((1, tile, KV_W), cur),
            pl.BlockSpec((1, tile, KV_W), lambda bi, m, sk: (bi, m, 1)),
            pl.BlockSpec((1, blk, KV_W), prev),
            pl.BlockSpec((1, blk, KV_W), lambda bi, m, sk: (bi, jnp.maximum(m * SWA_BLOCKS - 1, 0), 1)),
            pl.BlockSpec((1, tile, HEAD_DIM), cur),
            pl.BlockSpec((1, tile, HEAD_DIM), cur),
            pl.BlockSpec((1, blk, HEAD_DIM), prev),
            pl.BlockSpec((1, blk, HEAD_DIM), prev),
            pl.BlockSpec((HEAD_DIM, HEAD_DIM), lambda bi, m, sk: (0, 0)),
        ],
        out_specs=pl.BlockSpec((1, tile, SELF_W), cur),
    )
    return pl.pallas_call(
        _swa_kernel,
        out_shape=jax.ShapeDtypeStruct((b, s, SELF_W), jnp.bfloat16),
        grid_spec=grid_spec,
        compiler_params=_params("parallel", "parallel"),
        name="swa_attention",
    )(sink, q, kvx, kvx, kvx, kvx, cos_t, sin_t, cos_t, sin_t, rot)


def kernel(x, mem, positions, norm_mix, norm_mem, norm_mlp, w_mem_kv, pool_w_in, pool_w_group, pool_scale,
           pool_w_out, attn_w_in, attn_sink, attn_w_out, mlp_w1, mlp_w2, final_norm):
    b, s, d = x.shape
    t = b * s
    bf16 = jnp.bfloat16
    h = x.reshape(t, d)
    mem2 = mem.reshape(b * MEM_LEN, d)
    cos_t, sin_t = rope_tables(positions)

    hb = pss = None
    for i in range(2):
        memn = rmsnorm(mem2, norm_mem[i], bf16)
        mem_kv = matmul_rows(memn, w_mem_kv, i).reshape(b, MEM_LEN, 2 * XA_W)
        if i == 0:
            hn = rmsnorm(h, norm_mix[0], bf16)
            proj = matmul_wres([hn], pool_w_in, 0, SELF_W + XA_W).reshape(b, s, -1)
            y_self = pool_mixer(proj, pool_w_group[0].astype(bf16), pool_scale[0])
            y_mem = mem_xattn(proj, SELF_W, mem_kv)
            w_out = pool_w_out
        else:
            norm = (norm_mix[i], pss)
            q = matmul_wres([hb], attn_w_in, 0, SELF_W, norm=norm).reshape(b, s, -1)
            kvx = matmul_wres([hb], attn_w_in[:, :, SELF_W:], 0, 2 * KV_W + XA_W, bn=2 * KV_W + XA_W, norm=norm)
            kvx = kvx.reshape(b, s, -1)
            y_self = swa_attention(q, kvx, attn_sink[0], cos_t, sin_t)
            y_mem = mem_xattn(kvx, 2 * KV_W, mem_kv)
            w_out = attn_w_out
        h, hb, pss = matmul_wres([y_self.reshape(t, SELF_W), y_mem.reshape(t, XA_W)], w_out, 0, d,
                                 epilogue="residual", res=h, out_dtype=jnp.float32, emit_norm=True,
                                 vmem_limit_bytes=VMEM_LIMIT_BYTES_LARGE)
        a, w2b = matmul_wres([hb], mlp_w1, i, mlp_w1.shape[2], epilogue="relu2", norm=(norm_mlp[i], pss),
                             side=(mlp_w2, i))
        if i == 0:
            h, hb, pss = matmul_kacc(a, w2b, h, emit_norm=True)
        else:
            h = matmul_kacc(a, w2b, h)
    return rmsnorm(h, final_norm, jnp.float32).reshape(b, s, d)
```

```python
import functools

import jax
import jax.numpy as jnp
from jax import lax
from jax.experimental import pallas as pl
from jax.experimental.pallas import tpu as pltpu

D_MODEL = 4096
SELF_W = 3072
XA_W = 1024
XA_HEADS = 4
XA_HEAD_DIM = 256
MEM_LEN = 256
POOL_WINDOWS = (2, 4, 8, 16)
POOL_GROUP = 768
HEAD_DIM = 128
N_Q_HEADS = 24
GQA_GROUP = 8
N_KV_HEADS = 3
KV_W = 384
WINDOW = 128
ROT_DIM = 32
ROPE_THETA = 500000.0
EPS = 1e-6
NEG = -1e30

VMEM_LIMIT_BYTES = 56 * 1024 * 1024
VMEM_LIMIT_BYTES_LARGE = 60 * 1024 * 1024
LANES = 128
MM_ROW_CHUNK = 512
SWA_BLOCKS = 4
POOL_HALO = 16


def _params(*sem, vmem_limit_bytes=VMEM_LIMIT_BYTES):
    return pltpu.CompilerParams(dimension_semantics=sem, vmem_limit_bytes=vmem_limit_bytes)


def _rmsnorm_kernel(x_ref, g_ref, o_ref):
    x = x_ref[...]
    ms = jnp.mean(x * x, axis=-1, keepdims=True)
    o_ref[...] = (x * lax.rsqrt(ms + EPS) * g_ref[...]).astype(o_ref.dtype)


def rmsnorm(x, g, out_dtype, rows=512):
    r, d = x.shape
    return pl.pallas_call(
        _rmsnorm_kernel,
        out_shape=jax.ShapeDtypeStruct((r, d), out_dtype),
        grid=(r // rows,),
        in_specs=[pl.BlockSpec((rows, d), lambda i: (i, 0)),
                  pl.BlockSpec((1, d), lambda i: (0, 0))],
        out_specs=pl.BlockSpec((rows, d), lambda i: (i, 0)),
        compiler_params=_params("parallel"),
        name="rmsnorm",
    )(x, g.reshape(1, d))


def _epilogue(acc, epilogue, res_ref, o_ref, rs):
    if epilogue == "relu2":
        acc = jnp.square(jnp.maximum(acc, 0.0))
    elif epilogue == "residual":
        acc = res_ref[rs, :] + acc
    o_ref[rs, :] = acc.astype(o_ref.dtype)
    return acc


def _lane_group_sumsq(h_new):
    sq = h_new * h_new
    pss = sq[:, :LANES]
    for c in range(1, sq.shape[1] // LANES):
        pss = pss + sq[:, c * LANES:(c + 1) * LANES]
    return pss


def _row_rstd(pss_ref, d):
    return lax.rsqrt(jnp.sum(pss_ref[...], axis=-1, keepdims=True) * (1.0 / d) + EPS)


def _mm_wres_kernel(*refs, n_a, k_splits, chunk, n_j, row_chunk, epilogue, normed, emit_norm, side):
    refs = list(refs)
    a_refs = [refs.pop(0) for _ in range(n_a)]
    w_ref = refs.pop(0)
    g_ref = refs.pop(0) if normed else None
    pss_in_ref = refs.pop(0) if normed else None
    res_ref = refs.pop(0) if epilogue == "residual" else None
    side_in_ref = refs.pop(0) if side else None
    o_ref = refs.pop(0)
    hb_ref = refs.pop(0) if emit_norm else None
    pss_out_ref = refs.pop(0) if emit_norm else None
    side_out_ref = refs.pop(0) if side else None
    (wbf_ref,) = refs
    jp = pl.program_id(0)
    i = pl.program_id(1)
    n_slots = wbf_ref.shape[0]

    @pl.when(jp < n_j)
    def _():
        row0 = pl.multiple_of(i * chunk, chunk)
        w = w_ref[0]
        if normed:
            w = w * g_ref[...]
        wbf_ref[jp % n_slots, pl.ds(row0, chunk), :] = w.astype(jnp.bfloat16)

    @pl.when(jp > 0)
    def _():
        use_slot = (jp - 1) % n_slots
        rstd = _row_rstd(pss_in_ref, wbf_ref.shape[1]) if normed else None
        for r0 in range(0, o_ref.shape[0], row_chunk):
            rs = slice(r0, r0 + row_chunk)
            acc = None
            off = 0
            for a_ref, kw in zip(a_refs, k_splits):
                part = jnp.dot(a_ref[rs, :], wbf_ref[use_slot, pl.ds(off, kw), :], preferred_element_type=jnp.float32)
                acc = part if acc is None else acc + part
                off += kw
            if normed:
                acc = acc * rstd[rs]
            total = _epilogue(acc, epilogue, res_ref, o_ref, rs)
            if emit_norm:
                hb_ref[rs, :] = total.astype(hb_ref.dtype)
                pss_out_ref[rs, :] = _lane_group_sumsq(total)
        if side:
            side_out_ref[...] = side_in_ref[...].astype(side_out_ref.dtype)


def matmul_wres(a_list, w, layer, n_out, *, epilogue="cast", res=None, out_dtype=jnp.bfloat16, bm=1024, bn=1024,
                col0=0, norm=None, emit_norm=False, side=None, vmem_limit_bytes=VMEM_LIMIT_BYTES):
    m = a_list[0].shape[0]
    k_total = w.shape[1]
    k_splits = tuple(a.shape[1] for a in a_list)
    n_i, n_j = m // bm, n_out // bn
    chunk = k_total // n_i
    assert sum(k_splits) == k_total and n_i * bm == m and n_j * bn == n_out and chunk * n_i == k_total
    assert chunk % 16 == 0 and (not emit_norm or epilogue == "residual")

    def row_blk(jp, i):
        return jnp.where(jp == 0, 0, i)

    def stage_blk(jp, i):
        return jnp.where(jp < n_j, i, 0)

    o_map = lambda jp, i: (row_blk(jp, i), jnp.maximum(jp - 1, 0))
    in_specs = [pl.BlockSpec((bm, kw), lambda jp, i: (row_blk(jp, i), 0)) for kw in k_splits]
    in_specs.append(pl.BlockSpec((pl.Element(1), pl.Element(chunk), pl.Element(bn)),
                                 lambda jp, i: (layer, pl.multiple_of(stage_blk(jp, i) * chunk, chunk),
                                                pl.multiple_of(col0 + jnp.minimum(jp, n_j - 1) * bn, LANES))))
    args = list(a_list) + [w]
    if norm is not None:
        gain, pss = norm
        in_specs.append(pl.BlockSpec((chunk, 1), lambda jp, i: (stage_blk(jp, i), 0)))
        in_specs.append(pl.BlockSpec((bm, pss.shape[1]), lambda jp, i: (row_blk(jp, i), 0)))
        args += [gain.reshape(k_total, 1), pss]
    if epilogue == "residual":
        in_specs.append(pl.BlockSpec((bm, bn), o_map))
        args.append(res)
    out_shape = [jax.ShapeDtypeStruct((m, n_out), out_dtype)]
    out_specs = [pl.BlockSpec((bm, bn), o_map)]
    if emit_norm:
        out_shape += [jax.ShapeDtypeStruct((m, n_out), jnp.bfloat16),
                      jax.ShapeDtypeStruct((m, n_j * LANES), jnp.float32)]
        out_specs += [pl.BlockSpec((bm, bn), o_map), pl.BlockSpec((bm, LANES), o_map)]
    if side is not None:
        w_side, layer2 = side
        _, k2, n2 = w_side.shape
        rows2 = k2 // (n_j * n_i)
        assert rows2 * n_j * n_i == k2 and rows2 % 16 == 0
        slab = lambda jp, i: jnp.where(jp == 0, 0, (jp - 1) * n_i + i)
        in_specs.append(pl.BlockSpec((None, rows2, n2), lambda jp, i: (layer2, slab(jp, i), 0)))
        args.append(w_side)
        out_shape.append(jax.ShapeDtypeStruct((k2, n2), jnp.bfloat16))
        out_specs.append(pl.BlockSpec((rows2, n2), lambda jp, i: (slab(jp, i), 0)))
    outs = pl.pallas_call(
        functools.partial(_mm_wres_kernel, n_a=len(a_list), k_splits=k_splits, chunk=chunk, n_j=n_j,
                          row_chunk=min(bm, MM_ROW_CHUNK), epilogue=epilogue,
                          normed=norm is not None, emit_norm=emit_norm, side=side is not None),
        out_shape=out_shape,
        grid=(n_j + 1, n_i),
        in_specs=in_specs,
        out_specs=out_specs,
        scratch_shapes=[pltpu.VMEM((min(2, n_j), k_total, bn), jnp.bfloat16)],
        compiler_params=_params("arbitrary", "arbitrary", vmem_limit_bytes=vmem_limit_bytes),
        name="mm_wres_" + epilogue,
    )(*args)
    return outs[0] if len(outs) == 1 else tuple(outs)


def _mm_kacc_kernel(a_ref, w_ref, res_ref, o_ref, *norm_refs, nk):
    k = pl.program_id(2)

    @pl.when(k == 0)
    def _():
        o_ref[...] = res_ref[...]

    row_slices = [slice(r0, r0 + MM_ROW_CHUNK) for r0 in range(0, o_ref.shape[0], MM_ROW_CHUNK)]

    def partial_product(rs):
        return jnp.dot(a_ref[rs, :], w_ref[...], preferred_element_type=jnp.float32)

    def accumulate():
        for rs in row_slices:
            o_ref[rs, :] += partial_product(rs)

    if not norm_refs:
        accumulate()
    else:
        pl.when(k < nk - 1)(accumulate)

        @pl.when(k == nk - 1)
        def _():
            hb_ref, pss_ref = norm_refs
            for rs in row_slices:
                total = o_ref[rs, :] + partial_product(rs)
                o_ref[rs, :] = total
                hb_ref[rs, :] = total.astype(hb_ref.dtype)
                pss_ref[rs, :] = _lane_group_sumsq(total)


def matmul_kacc(a, w, res, *, emit_norm=False, bm=1024, bn=1024):
    m, k_total = a.shape
    n = w.shape[1]
    bk, vmem_limit = (4096, VMEM_LIMIT_BYTES_LARGE) if emit_norm else (4096, VMEM_LIMIT_BYTES_LARGE)
    assert m % bm == 0 and n % bn == 0 and k_total % bk == 0
    o_map = lambda j, i, k: (i, j)
    out_shape = [jax.ShapeDtypeStruct((m, n), jnp.float32)]
    out_specs = [pl.BlockSpec((bm, bn), o_map)]
    if emit_norm:
        out_shape += [jax.ShapeDtypeStruct((m, n), jnp.bfloat16),
                      jax.ShapeDtypeStruct((m, (n // bn) * LANES), jnp.float32)]
        out_specs += [pl.BlockSpec((bm, bn), o_map), pl.BlockSpec((bm, LANES), o_map)]
    outs = pl.pallas_call(
        functools.partial(_mm_kacc_kernel, nk=k_total // bk),
        out_shape=out_shape,
        grid=(n // bn, m // bm, k_total // bk),
        in_specs=[pl.BlockSpec((bm, bk), lambda j, i, k: (i, k)),
                  pl.BlockSpec((bk, bn), lambda j, i, k: (k, j)),
                  pl.BlockSpec((bm, bn), o_map)],
        out_specs=out_specs,
        compiler_params=_params("parallel", "parallel", "arbitrary", vmem_limit_bytes=vmem_limit),
        name="mm_kacc_residual",
    )(a, w, res)
    return outs[0] if len(outs) == 1 else tuple(outs)


def _mem_proj_kernel(x_ref, g_ref, w_ref, o_ref, xn_ref):
    @pl.when(pl.program_id(1) == 0)
    def _():
        x = x_ref[...]
        ms = jnp.mean(x * x, axis=-1, keepdims=True)
        xn_ref[...] = (x * lax.rsqrt(ms + EPS) * g_ref[...]).astype(xn_ref.dtype)

    o_ref[...] = jnp.dot(xn_ref[...], w_ref[...].astype(jnp.bfloat16),
                         preferred_element_type=jnp.float32).astype(o_ref.dtype)


def mem_projections(x, g, w, *, bn=512):
    m, k_total = x.shape
    n_layers, _, n = w.shape
    assert n % bn == 0
    return pl.pallas_call(
        _mem_proj_kernel,
        out_shape=jax.ShapeDtypeStruct((n_layers, m, n), jnp.bfloat16),
        grid=(n_layers, n // bn),
        in_specs=[pl.BlockSpec((m, k_total), lambda l, j: (0, 0)),
                  pl.BlockSpec((None, 1, k_total), lambda l, j: (l, 0, 0)),
                  pl.BlockSpec((None, k_total, bn), lambda l, j: (l, 0, j))],
        out_specs=pl.BlockSpec((None, m, bn), lambda l, j: (l, 0, j)),
        scratch_shapes=[pltpu.VMEM((m, k_total), jnp.bfloat16)],
        compiler_params=_params("arbitrary", "arbitrary"),
        name="mem_proj",
    )(x, g.reshape(n_layers, 1, k_total), w)


def _pool_kernel(u_ref, halo_ref, w_ref, scale_ref, o_ref, *, ts):
    i = pl.program_id(1)
    cur = u_ref[0].astype(jnp.float32)
    halo = halo_ref[0].astype(jnp.float32)
    halo = jnp.where(i > 0, halo, 0.0)
    t1 = (i * ts + 1 + lax.broadcasted_iota(jnp.int32, (ts, 1), 0)).astype(jnp.float32)
    for g, win in enumerate(POOL_WINDOWS):
        c0, c1 = g * POOL_GROUP, (g + 1) * POOL_GROUP
        x = jnp.concatenate([halo[:, c0:c1], cur[:, c0:c1]], axis=0)
        s = x
        d = 1
        while d < win:
            s = s[d:] + s[:-d]
            d *= 2
        s = s[POOL_HALO - (win - 1):]
        mean = s / jnp.minimum(t1, float(win))
        p = (mean - cur[:, c0:c1]).astype(jnp.bfloat16)
        y = jnp.dot(p, w_ref[g], preferred_element_type=jnp.float32)
        o_ref[0, :, c0:c1] = (y * scale_ref[:, c0:c1]).astype(o_ref.dtype)


def pool_mixer(proj, w_group, scale, ts=512):
    b, s, _ = proj.shape
    hb = ts // POOL_HALO
    return pl.pallas_call(
        functools.partial(_pool_kernel, ts=ts),
        out_shape=jax.ShapeDtypeStruct((b, s, SELF_W), jnp.bfloat16),
        grid=(b, s // ts),
        in_specs=[
            pl.BlockSpec((1, ts, SELF_W), lambda bi, i: (bi, i, 0)),
            pl.BlockSpec((1, POOL_HALO, SELF_W), lambda bi, i: (bi, jnp.maximum(i * hb - 1, 0), 0)),
            pl.BlockSpec((len(POOL_WINDOWS), POOL_GROUP, POOL_GROUP), lambda bi, i: (0, 0, 0)),
            pl.BlockSpec((1, SELF_W), lambda bi, i: (0, 0)),
        ],
        out_specs=pl.BlockSpec((1, ts, SELF_W), lambda bi, i: (bi, i, 0)),
        compiler_params=_params("parallel", "parallel"),
        name="pool_mixer",
    )(proj, proj, w_group, scale.reshape(1, SELF_W))


def _xattn_kernel(*refs):
    q_refs, kv_ref, o_ref = refs[:XA_HEADS], refs[XA_HEADS], refs[XA_HEADS + 1]
    for h in range(XA_HEADS):
        hs = slice(h * XA_HEAD_DIM, (h + 1) * XA_HEAD_DIM)
        q = q_refs[h][0]
        k = kv_ref[0, :, hs]
        v = kv_ref[0, :, XA_W + h * XA_HEAD_DIM:XA_W + (h + 1) * XA_HEAD_DIM]
        s = lax.dot_general(q, k, (((1,), (1,)), ((), ())), preferred_element_type=jnp.float32)
        s = s * (XA_HEAD_DIM ** -0.5)
        m = jnp.max(s, axis=-1, keepdims=True)
        e = jnp.exp(s - m)
        l = jnp.sum(e, axis=-1, keepdims=True)
        o = jnp.dot(e.astype(jnp.bfloat16), v, preferred_element_type=jnp.float32)
        o_ref[0, :, hs] = (o / l).astype(o_ref.dtype)


def mem_xattn(proj, q_col0, mem_kv, ts=1024):
    b, s, _ = proj.shape
    qb = q_col0 // XA_HEAD_DIM
    assert qb * XA_HEAD_DIM == q_col0
    q_specs = [pl.BlockSpec((1, ts, XA_HEAD_DIM), functools.partial(lambda bi, i, h: (bi, i, qb + h), h=h))
               for h in range(XA_HEADS)]
    return pl.pallas_call(
        _xattn_kernel,
        out_shape=jax.ShapeDtypeStruct((b, s, XA_W), jnp.bfloat16),
        grid=(b, s // ts),
        in_specs=q_specs + [pl.BlockSpec((1, MEM_LEN, 2 * XA_W), lambda bi, i: (bi, 0, 0))],
        out_specs=pl.BlockSpec((1, ts, XA_W), lambda bi, i: (bi, i, 0)),
        compiler_params=_params("parallel", "parallel"),
        name="mem_xattn",
    )(*([proj] * XA_HEADS), mem_kv)


def _rope_table_kernel(pos_ref, freq_ref, sign_ref, cos_ref, sin_ref):
    ang = pos_ref[0].astype(jnp.float32) * freq_ref[...]
    cos_ref[0] = jnp.cos(ang)
    sin_ref[0] = jnp.sin(ang) * sign_ref[...]


def rope_tables(positions, ts=1024):
    b, s = positions.shape
    half = ROT_DIM // 2
    inv_freq = ROPE_THETA ** (-jnp.arange(0, ROT_DIM, 2, dtype=jnp.float32) / ROT_DIM)
    freq = jnp.concatenate([inv_freq, inv_freq, jnp.zeros((HEAD_DIM - ROT_DIM,), jnp.float32)]).reshape(1, HEAD_DIM)
    sign = jnp.concatenate([-jnp.ones((half,), jnp.float32), jnp.ones((half,), jnp.float32),
                            jnp.zeros((HEAD_DIM - ROT_DIM,), jnp.float32)]).reshape(1, HEAD_DIM)
    out = jax.ShapeDtypeStruct((b, s, HEAD_DIM), jnp.float32)
    return pl.pallas_call(
        _rope_table_kernel,
        out_shape=(out, out),
        grid=(b, s // ts),
        in_specs=[pl.BlockSpec((1, ts, 1), lambda bi, i: (bi, i, 0)),
                  pl.BlockSpec((1, HEAD_DIM), lambda bi, i: (0, 0)),
                  pl.BlockSpec((1, HEAD_DIM), lambda bi, i: (0, 0))],
        out_specs=(pl.BlockSpec((1, ts, HEAD_DIM), lambda bi, i: (bi, i, 0)),
                   pl.BlockSpec((1, ts, HEAD_DIM), lambda bi, i: (bi, i, 0))),
        compiler_params=_params("parallel", "parallel"),
        name="rope_tables",
    )(positions.reshape(b, s, 1), freq, sign)


def _rope(xb, cos_t, sin_t, rot):
    partner = jnp.dot(xb, rot, preferred_element_type=jnp.float32)
    return xb.astype(jnp.float32) * cos_t + partner * sin_t


def _swa_kernel(sink_ref, q_ref, kc_ref, vc_ref, kp_ref, vp_ref, cosc_ref, sinc_ref, cosp_ref, sinp_ref, rot_ref,
                o_ref):
    m_step = pl.program_id(1)
    blk = WINDOW
    cols = GQA_GROUP * blk
    rot = rot_ref[...]
    cos_b = [cosp_ref[0]] + [cosc_ref[0, t * blk:(t + 1) * blk] for t in range(SWA_BLOCKS)]
    sin_b = [sinp_ref[0]] + [sinc_ref[0, t * blk:(t + 1) * blk] for t in range(SWA_BLOCKS)]

    kj = lax.broadcasted_iota(jnp.int32, (blk, cols), 0)
    col = lax.broadcasted_iota(jnp.int32, (blk, cols), 1)
    cur_live = kj <= col % blk
    first_live = jnp.logical_or(cur_live, m_step > 0)
    col_group = lax.broadcasted_iota(jnp.int32, (1, cols), 1) // blk

    for h in range(N_KV_HEADS):
        hs = slice(h * HEAD_DIM, (h + 1) * HEAD_DIM)
        k_b = [_rope(kp_ref[0, :, hs], cos_b[0], sin_b[0], rot).astype(jnp.bfloat16)]
        v_b = [vp_ref[0, :, hs]]
        for t in range(SWA_BLOCKS):
            rows = slice(t * blk, (t + 1) * blk)
            k_b.append(_rope(kc_ref[0, rows, hs], cos_b[t + 1], sin_b[t + 1], rot).astype(jnp.bfloat16))
            v_b.append(vc_ref[0, rows, hs])
        sink = jnp.zeros((1, cols), jnp.float32)
        for g in range(GQA_GROUP):
            sink = jnp.where(col_group == g, sink_ref[h * GQA_GROUP + g], sink)
        for t in range(SWA_BLOCKS):
            rows = slice(t * blk, (t + 1) * blk)
            k = jnp.concatenate([k_b[t], k_b[t + 1]], axis=0)
            v = jnp.concatenate([v_b[t], v_b[t + 1]], axis=0)
            q_parts = []
            for g in range(GQA_GROUP):
                c0 = (h * GQA_GROUP + g) * HEAD_DIM
                q_parts.append(_rope(q_ref[0, rows, c0:c0 + HEAD_DIM], cos_b[t + 1], sin_b[t + 1], rot))
            q = jnp.concatenate(q_parts, axis=0).astype(jnp.bfloat16)
            st = lax.dot_general(k, q, (((1,), (1,)), ((), ())), preferred_element_type=jnp.float32)
            s = jnp.where(cur_live, st[blk:], st[:blk]) * (HEAD_DIM ** -0.5)
            if t == 0:
                s = jnp.where(first_live, s, NEG)
            m = jnp.maximum(jnp.max(s, axis=0, keepdims=True), sink)
            e = jnp.exp(s - m)
            l = jnp.sum(e, axis=0, keepdims=True) + jnp.exp(sink - m)
            p = e * (1.0 / l)
            pt = jnp.concatenate([jnp.where(cur_live, 0.0, p), jnp.where(cur_live, p, 0.0)], axis=0)
            o = lax.dot_general(pt.astype(jnp.bfloat16), v, (((0,), (0,)), ((), ())),
                                preferred_element_type=jnp.float32)
            for g in range(GQA_GROUP):
                c0 = (h * GQA_GROUP + g) * HEAD_DIM
                o_ref[0, rows, c0:c0 + HEAD_DIM] = o[g * blk:(g + 1) * blk].astype(o_ref.dtype)


def swa_attention(q, kvx, sink, cos_t, sin_t):
    b, s, _ = q.shape
    blk = WINDOW
    tile = SWA_BLOCKS * blk
    half = ROT_DIM // 2
    src_lane = lax.broadcasted_iota(jnp.int32, (HEAD_DIM, HEAD_DIM), 0)
    dst_lane = lax.broadcasted_iota(jnp.int32, (HEAD_DIM, HEAD_DIM), 1)
    partner_of_dst = jnp.where(dst_lane < half, dst_lane + half, dst_lane - half)
    rot = jnp.logical_and(dst_lane < ROT_DIM, src_lane == partner_of_dst).astype(jnp.bfloat16)
    cur = lambda bi, m, sk: (bi, m, 0)
    prev = lambda bi, m, sk: (bi, jnp.maximum(m * SWA_BLOCKS - 1, 0), 0)
    grid_spec = pltpu.PrefetchScalarGridSpec(
        num_scalar_prefetch=1,
        grid=(b, s // tile),
        in_specs=[
            pl.BlockSpec((1, tile, SELF_W), cur),
            pl.BlockSpec((1, tile, KV_W), cur),
            pl.BlockSpec((1, tile, KV_W), lambda bi, m, sk: (bi, m, 1)),
            pl.BlockSpec((1, blk, KV_W), prev),
            pl.BlockSpec((1, blk, KV_W), lambda bi, m, sk: (bi, jnp.maximum(m * SWA_BLOCKS - 1, 0), 1)),
            pl.BlockSpec((1, tile, HEAD_DIM), cur),
            pl.BlockSpec((1, tile, HEAD_DIM), cur),
            pl.BlockSpec((1, blk, HEAD_DIM), prev),
            pl.BlockSpec((1, blk, HEAD_DIM), prev),
            pl.BlockSpec((HEAD_DIM, HEAD_DIM), lambda bi, m, sk: (0, 0)),
        ],
        out_specs=pl.BlockSpec((1, tile, SELF_W), cur),
    )
    return pl.pallas_call(
        _swa_kernel,
        out_shape=jax.ShapeDtypeStruct((b, s, SELF_W), jnp.bfloat16),
        grid_spec=grid_spec,
        compiler_params=_params("parallel", "parallel"),
        name="swa_attention",
    )(sink, q, kvx, kvx, kvx, kvx, cos_t, sin_t, cos_t, sin_t, rot)


def kernel(x, mem, positions, norm_mix, norm_mem, norm_mlp, w_mem_kv, pool_w_in, pool_w_group, pool_scale,
           pool_w_out, attn_w_in, attn_sink, attn_w_out, mlp_w1, mlp_w2, final_norm):
    b, s, d = x.shape
    t = b * s
    bf16 = jnp.bfloat16
    h = x.reshape(t, d)
    cos_t, sin_t = rope_tables(positions)
    mem_kv_all = mem_projections(mem.reshape(b * MEM_LEN, d), norm_mem, w_mem_kv)
    hb = pss = None
    for i in range(2):
        mem_kv = mem_kv_all[i].reshape(b, MEM_LEN, 2 * XA_W)
        if i == 0:
            hn = rmsnorm(h, norm_mix[0], bf16)
            proj = matmul_wres([hn], pool_w_in, 0, SELF_W + XA_W).reshape(b, s, -1)
            y_self = pool_mixer(proj, pool_w_group[0].astype(bf16), pool_scale[0])
            y_mem = mem_xattn(proj, SELF_W, mem_kv)
            w_out = pool_w_out
        else:
            norm = (norm_mix[i], pss)
            q = matmul_wres([hb], attn_w_in, 0, SELF_W, norm=norm).reshape(b, s, -1)
            kvx = matmul_wres([hb], attn_w_in, 0, 2 * KV_W + XA_W, bn=2 * KV_W + XA_W, col0=SELF_W, norm=norm)
            kvx = kvx.reshape(b, s, -1)
            y_self = swa_attention(q, kvx, attn_sink[0], cos_t, sin_t)
            y_mem = mem_xattn(kvx, 2 * KV_W, mem_kv)
            w_out = attn_w_out
        h, hb, pss = matmul_wres([y_self.reshape(t, SELF_W), y_mem.reshape(t, XA_W)], w_out, 0, d,
                                 epilogue="residual", res=h, out_dtype=jnp.float32, emit_norm=True,
                                 vmem_limit_bytes=VMEM_LIMIT_BYTES_LARGE)
        a, w2b = matmul_wres([hb], mlp_w1, i, mlp_w1.shape[2], epilogue="relu2", norm=(norm_mlp[i], pss),
                             side=(mlp_w2, i))
        if i == 0:
            h, hb, pss = matmul_kacc(a, w2b, h, emit_norm=True)
        else:
            h = matmul_kacc(a, w2b, h)
    return rmsnorm(h, final_norm, jnp.float32).reshape(b, s, d)
```

```python
import functools

import jax
import jax.numpy as jnp
from jax import lax
from jax.experimental import pallas as pl
from jax.experimental.pallas import tpu as pltpu

D_MODEL = 4096
SELF_W = 3072
XA_W = 1024
XA_HEADS = 4
XA_HEAD_DIM = 256
MEM_LEN = 256
POOL_WINDOWS = (2, 4, 8, 16)
POOL_GROUP = 768
HEAD_DIM = 128
N_Q_HEADS = 24
GQA_GROUP = 8
N_KV_HEADS = 3
KV_W = 384
WINDOW = 128
ROT_DIM = 32
ROPE_THETA = 500000.0
EPS = 1e-6
NEG = -1e30

VMEM_LIMIT_BYTES = 56 * 1024 * 1024
VMEM_LIMIT_BYTES_LARGE = 60 * 1024 * 1024
LANES = 128
MM_ROW_CHUNK = 512
SWA_BLOCKS = 8
POOL_HALO = 16


def _params(*sem, vmem_limit_bytes=VMEM_LIMIT_BYTES):
    return pltpu.CompilerParams(dimension_semantics=sem, vmem_limit_bytes=vmem_limit_bytes)


def _rmsnorm_kernel(x_ref, g_ref, o_ref):
    x = x_ref[...]
    ms = jnp.mean(x * x, axis=-1, keepdims=True)
    o_ref[...] = (x * lax.rsqrt(ms + EPS) * g_ref[...]).astype(o_ref.dtype)


def rmsnorm(x, g, out_dtype, rows=512):
    r, d = x.shape
    return pl.pallas_call(
        _rmsnorm_kernel,
        out_shape=jax.ShapeDtypeStruct((r, d), out_dtype),
        grid=(r // rows,),
        in_specs=[pl.BlockSpec((rows, d), lambda i: (i, 0)),
                  pl.BlockSpec((1, d), lambda i: (0, 0))],
        out_specs=pl.BlockSpec((rows, d), lambda i: (i, 0)),
        compiler_params=_params("parallel"),
        name="rmsnorm",
    )(x, g.reshape(1, d))


def _epilogue(acc, epilogue, res_ref, o_ref, rs):
    if epilogue == "relu2":
        acc = jnp.square(jnp.maximum(acc, 0.0))
    elif epilogue == "residual":
        acc = res_ref[rs, :] + acc
    o_ref[rs, :] = acc.astype(o_ref.dtype)
    return acc


def _lane_group_sumsq(h_new):
    sq = h_new * h_new
    pss = sq[:, :LANES]
    for c in range(1, sq.shape[1] // LANES):
        pss = pss + sq[:, c * LANES:(c + 1) * LANES]
    return pss


def _row_rstd(pss_ref, d):
    return lax.rsqrt(jnp.sum(pss_ref[...], axis=-1, keepdims=True) * (1.0 / d) + EPS)


def _mm_wres_kernel(*refs, n_a, k_splits, chunk, n_j, row_chunk, epilogue, normed, emit_norm, side):
    refs = list(refs)
    a_refs = [refs.pop(0) for _ in range(n_a)]
    w_ref = refs.pop(0)
    g_ref = refs.pop(0) if normed else None
    pss_in_ref = refs.pop(0) if normed else None
    res_ref = refs.pop(0) if epilogue == "residual" else None
    side_in_ref = refs.pop(0) if side else None
    o_ref = refs.pop(0)
    hb_ref = refs.pop(0) if emit_norm else None
    pss_out_ref = refs.pop(0) if emit_norm else None
    side_out_ref = refs.pop(0) if side else None
    (wbf_ref,) = refs
    jp = pl.program_id(0)
    i = pl.program_id(1)
    n_slots = wbf_ref.shape[0]

    @pl.when(jp < n_j)
    def _():
        row0 = pl.multiple_of(i * chunk, chunk)
        w = w_ref[0]
        if normed:
            w = w * g_ref[...]
        wbf_ref[jp % n_slots, pl.ds(row0, chunk), :] = w.astype(jnp.bfloat16)

    @pl.when(jp > 0)
    def _():
        use_slot = (jp - 1) % n_slots
        rstd = _row_rstd(pss_in_ref, wbf_ref.shape[1]) if normed else None
        for r0 in range(0, o_ref.shape[0], row_chunk):
            rs = slice(r0, r0 + row_chunk)
            acc = None
            off = 0
            for a_ref, kw in zip(a_refs, k_splits):
                part = jnp.dot(a_ref[rs, :], wbf_ref[use_slot, pl.ds(off, kw), :], preferred_element_type=jnp.float32)
                acc = part if acc is None else acc + part
                off += kw
            if normed:
                acc = acc * rstd[rs]
            total = _epilogue(acc, epilogue, res_ref, o_ref, rs)
            if emit_norm:
                hb_ref[rs, :] = total.astype(hb_ref.dtype)
                pss_out_ref[rs, :] = _lane_group_sumsq(total)
        if side:
            side_out_ref[...] = side_in_ref[...].astype(side_out_ref.dtype)


def matmul_wres(a_list, w, layer, n_out, *, epilogue="cast", res=None, out_dtype=jnp.bfloat16, bm=1024, bn=1024,
                col0=0, norm=None, emit_norm=False, side=None, vmem_limit_bytes=VMEM_LIMIT_BYTES):
    m = a_list[0].shape[0]
    k_total = w.shape[1]
    k_splits = tuple(a.shape[1] for a in a_list)
    n_i, n_j = m // bm, n_out // bn
    chunk = k_total // n_i
    assert sum(k_splits) == k_total and n_i * bm == m and n_j * bn == n_out and chunk * n_i == k_total
    assert chunk % 16 == 0 and (not emit_norm or epilogue == "residual")

    def row_blk(jp, i):
        return jnp.where(jp == 0, 0, i)

    def stage_blk(jp, i):
        return jnp.where(jp < n_j, i, 0)

    o_map = lambda jp, i: (row_blk(jp, i), jnp.maximum(jp - 1, 0))
    in_specs = [pl.BlockSpec((bm, kw), lambda jp, i: (row_blk(jp, i), 0)) for kw in k_splits]
    in_specs.append(pl.BlockSpec((pl.Element(1), pl.Element(chunk), pl.Element(bn)),
                                 lambda jp, i: (layer, pl.multiple_of(stage_blk(jp, i) * chunk, chunk),
                                                pl.multiple_of(col0 + jnp.minimum(jp, n_j - 1) * bn, LANES))))
    args = list(a_list) + [w]
    if norm is not None:
        gain, pss = norm
        in_specs.append(pl.BlockSpec((chunk, 1), lambda jp, i: (stage_blk(jp, i), 0)))
        in_specs.append(pl.BlockSpec((bm, pss.shape[1]), lambda jp, i: (row_blk(jp, i), 0)))
        args += [gain.reshape(k_total, 1), pss]
    if epilogue == "residual":
        in_specs.append(pl.BlockSpec((bm, bn), o_map))
        args.append(res)
    out_shape = [jax.ShapeDtypeStruct((m, n_out), out_dtype)]
    out_specs = [pl.BlockSpec((bm, bn), o_map)]
    if emit_norm:
        out_shape += [jax.ShapeDtypeStruct((m, n_out), jnp.bfloat16),
                      jax.ShapeDtypeStruct((m, n_j * LANES), jnp.float32)]
        out_specs += [pl.BlockSpec((bm, bn), o_map), pl.BlockSpec((bm, LANES), o_map)]
    if side is not None:
        w_side, layer2 = side
        _, k2, n2 = w_side.shape
        rows2 = k2 // (n_j * n_i)
        assert rows2 * n_j * n_i == k2 and rows2 % 16 == 0
        slab = lambda jp, i: jnp.where(jp == 0, 0, (jp - 1) * n_i + i)
        in_specs.append(pl.BlockSpec((None, rows2, n2), lambda jp, i: (layer2, slab(jp, i), 0)))
        args.append(w_side)
        out_shape.append(jax.ShapeDtypeStruct((k2, n2), jnp.bfloat16))
        out_specs.append(pl.BlockSpec((rows2, n2), lambda jp, i: (slab(jp, i), 0)))
    outs = pl.pallas_call(
        functools.partial(_mm_wres_kernel, n_a=len(a_list), k_splits=k_splits, chunk=chunk, n_j=n_j,
                          row_chunk=min(bm, MM_ROW_CHUNK), epilogue=epilogue,
                          normed=norm is not None, emit_norm=emit_norm, side=side is not None),
        out_shape=out_shape,
        grid=(n_j + 1, n_i),
        in_specs=in_specs,
        out_specs=out_specs,
        scratch_shapes=[pltpu.VMEM((min(2, n_j), k_total, bn), jnp.bfloat16)],
        compiler_params=_params("arbitrary", "arbitrary", vmem_limit_bytes=vmem_limit_bytes),
        name="mm_wres_" + epilogue,
    )(*args)
    return outs[0] if len(outs) == 1 else tuple(outs)


def _mm_kacc_kernel(a_ref, w_ref, res_ref, o_ref, *norm_refs, nk):
    k = pl.program_id(2)

    @pl.when(k == 0)
    def _():
        o_ref[...] = res_ref[...]

    row_slices = [slice(r0, r0 + MM_ROW_CHUNK) for r0 in range(0, o_ref.shape[0], MM_ROW_CHUNK)]

    def partial_product(rs):
        return jnp.dot(a_ref[rs, :], w_ref[...], preferred_element_type=jnp.float32)

    def accumulate():
        for rs in row_slices:
            o_ref[rs, :] += partial_product(rs)

    if not norm_refs:
        accumulate()
    else:
        pl.when(k < nk - 1)(accumulate)

        @pl.when(k == nk - 1)
        def _():
            hb_ref, pss_ref = norm_refs
            for rs in row_slices:
                total = o_ref[rs, :] + partial_product(rs)
                o_ref[rs, :] = total
                hb_ref[rs, :] = total.astype(hb_ref.dtype)
                pss_ref[rs, :] = _lane_group_sumsq(total)


def matmul_kacc(a, w, res, *, emit_norm=False, bm=1024, bn=1024):
    m, k_total = a.shape
    n = w.shape[1]
    bk, vmem_limit = (4096, VMEM_LIMIT_BYTES_LARGE) if emit_norm else (4096, VMEM_LIMIT_BYTES_LARGE)
    assert m % bm == 0 and n % bn == 0 and k_total % bk == 0
    o_map = lambda j, i, k: (i, j)
    out_shape = [jax.ShapeDtypeStruct((m, n), jnp.float32)]
    out_specs = [pl.BlockSpec((bm, bn), o_map)]
    if emit_norm:
        out_shape += [jax.ShapeDtypeStruct((m, n), jnp.bfloat16),
                      jax.ShapeDtypeStruct((m, (n // bn) * LANES), jnp.float32)]
        out_specs += [pl.BlockSpec((bm, bn), o_map), pl.BlockSpec((bm, LANES), o_map)]
    outs = pl.pallas_call(
        functools.partial(_mm_kacc_kernel, nk=k_total // bk),
        out_shape=out_shape,
        grid=(n // bn, m // bm, k_total // bk),
        in_specs=[pl.BlockSpec((bm, bk), lambda j, i, k: (i, k)),
                  pl.BlockSpec((bk, bn), lambda j, i, k: (k, j)),
                  pl.BlockSpec((bm, bn), o_map)],
        out_specs=out_specs,
        compiler_params=_params("parallel", "parallel", "arbitrary", vmem_limit_bytes=vmem_limit),
        name="mm_kacc_residual",
    )(a, w, res)
    return outs[0] if len(outs) == 1 else tuple(outs)


def _mem_proj_kernel(x_ref, g_ref, w_ref, o_ref, xn_ref):
    @pl.when(pl.program_id(1) == 0)
    def _():
        x = x_ref[...]
        ms = jnp.mean(x * x, axis=-1, keepdims=True)
        xn_ref[...] = (x * lax.rsqrt(ms + EPS) * g_ref[...]).astype(xn_ref.dtype)

    o_ref[...] = jnp.dot(xn_ref[...], w_ref[...].astype(jnp.bfloat16),
                         preferred_element_type=jnp.float32).astype(o_ref.dtype)


def mem_projections(x, g, w, *, bn=512):
    m, k_total = x.shape
    n_layers, _, n = w.shape
    assert n % bn == 0
    return pl.pallas_call(
        _mem_proj_kernel,
        out_shape=jax.ShapeDtypeStruct((n_layers, m, n), jnp.bfloat16),
        grid=(n_layers, n // bn),
        in_specs=[pl.BlockSpec((m, k_total), lambda l, j: (0, 0)),
                  pl.BlockSpec((None, 1, k_total), lambda l, j: (l, 0, 0)),
                  pl.BlockSpec((None, k_total, bn), lambda l, j: (l, 0, j))],
        out_specs=pl.BlockSpec((None, m, bn), lambda l, j: (l, 0, j)),
        scratch_shapes=[pltpu.VMEM((m, k_total), jnp.bfloat16)],
        compiler_params=_params("arbitrary", "arbitrary"),
        name="mem_proj",
    )(x, g.reshape(n_layers, 1, k_total), w)


def _pool_kernel(u_ref, halo_ref, w_ref, scale_ref, o_ref, *, ts):
    i = pl.program_id(1)
    cur = u_ref[0].astype(jnp.float32)
    halo = halo_ref[0].astype(jnp.float32)
    halo = jnp.where(i > 0, halo, 0.0)
    t1 = (i * ts + 1 + lax.broadcasted_iota(jnp.int32, (ts, 1), 0)).astype(jnp.float32)
    for g, win in enumerate(POOL_WINDOWS):
        c0, c1 = g * POOL_GROUP, (g + 1) * POOL_GROUP
        x = jnp.concatenate([halo[:, c0:c1], cur[:, c0:c1]], axis=0)
        s = x
        d = 1
        while d < win:
            s = s[d:] + s[:-d]
            d *= 2
        s = s[POOL_HALO - (win - 1):]
        mean = s / jnp.minimum(t1, float(win))
        p = (mean - cur[:, c0:c1]).astype(jnp.bfloat16)
        y = jnp.dot(p, w_ref[g], preferred_element_type=jnp.float32)
        o_ref[0, :, c0:c1] = (y * scale_ref[:, c0:c1]).astype(o_ref.dtype)


def pool_mixer(proj, w_group, scale, ts=512):
    b, s, _ = proj.shape
    hb = ts // POOL_HALO
    return pl.pallas_call(
        functools.partial(_pool_kernel, ts=ts),
        out_shape=jax.ShapeDtypeStruct((b, s, SELF_W), jnp.bfloat16),
        grid=(b, s // ts),
        in_specs=[
            pl.BlockSpec((1, ts, SELF_W), lambda bi, i: (bi, i, 0)),
            pl.BlockSpec((1, POOL_HALO, SELF_W), lambda bi, i: (bi, jnp.maximum(i * hb - 1, 0), 0)),
            pl.BlockSpec((len(POOL_WINDOWS), POOL_GROUP, POOL_GROUP), lambda bi, i: (0, 0, 0)),
            pl.BlockSpec((1, SELF_W), lambda bi, i: (0, 0)),
        ],
        out_specs=pl.BlockSpec((1, ts, SELF_W), lambda bi, i: (bi, i, 0)),
        compiler_params=_params("parallel", "parallel"),
        name="pool_mixer",
    )(proj, proj, w_group, scale.reshape(1, SELF_W))


def _xattn_kernel(*refs):
    q_refs, kv_ref, o_ref = refs[:XA_HEADS], refs[XA_HEADS], refs[XA_HEADS + 1]
    for h in range(XA_HEADS):
        hs = slice(h * XA_HEAD_DIM, (h + 1) * XA_HEAD_DIM)
        q = q_refs[h][0]
        k = kv_ref[0, :, hs]
        v = kv_ref[0, :, XA_W + h * XA_HEAD_DIM:XA_W + (h + 1) * XA_HEAD_DIM]
        s = lax.dot_general(q, k, (((1,), (1,)), ((), ())), preferred_element_type=jnp.float32)
        s = s * (XA_HEAD_DIM ** -0.5)
        m = jnp.max(s, axis=-1, keepdims=True)
        e = jnp.exp(s - m)
        l = jnp.sum(e, axis=-1, keepdims=True)
        o = jnp.dot(e.astype(jnp.bfloat16), v, preferred_element_type=jnp.float32)
        o_ref[0, :, hs] = (o / l).astype(o_ref.dtype)


def mem_xattn(proj, q_col0, mem_kv, ts=1024):
    b, s, _ = proj.shape
    qb = q_col0 // XA_HEAD_DIM
    assert qb * XA_HEAD_DIM == q_col0
    q_specs = [pl.BlockSpec((1, ts, XA_HEAD_DIM), functools.partial(lambda bi, i, h: (bi, i, qb + h), h=h))
               for h in range(XA_HEADS)]
    return pl.pallas_call(
        _xattn_kernel,
        out_shape=jax.ShapeDtypeStruct((b, s, XA_W), jnp.bfloat16),
        grid=(b, s // ts),
        in_specs=q_specs + [pl.BlockSpec((1, MEM_LEN, 2 * XA_W), lambda bi, i: (bi, 0, 0))],
        out_specs=pl.BlockSpec((1, ts, XA_W), lambda bi, i: (bi, i, 0)),
        compiler_params=_params("parallel", "parallel"),
        name="mem_xattn",
    )(*([proj] * XA_HEADS), mem_kv)


def _norm_rope_kernel(x_ref, g_ref, pos_ref, freq_ref, sign_ref, o_ref, cos_ref, sin_ref):
    _rmsnorm_kernel(x_ref, g_ref, o_ref)
    ang = pos_ref[...].astype(jnp.float32) * freq_ref[...]
    cos_ref[...] = jnp.cos(ang)
    sin_ref[...] = jnp.sin(ang) * sign_ref[...]


def rmsnorm_and_rope_tables(x, g, positions, rows=512):
    t, d = x.shape
    half = ROT_DIM // 2
    inv_freq = ROPE_THETA ** (-jnp.arange(0, ROT_DIM, 2, dtype=jnp.float32) / ROT_DIM)
    freq = jnp.concatenate([inv_freq, inv_freq, jnp.zeros((HEAD_DIM - ROT_DIM,), jnp.float32)]).reshape(1, HEAD_DIM)
    sign = jnp.concatenate([-jnp.ones((half,), jnp.float32), jnp.ones((half,), jnp.float32),
                            jnp.zeros((HEAD_DIM - ROT_DIM,), jnp.float32)]).reshape(1, HEAD_DIM)
    table = jax.ShapeDtypeStruct((t, HEAD_DIM), jnp.float32)
    row_blk = lambda i: (i, 0)
    const = lambda i: (0, 0)
    return pl.pallas_call(
        _norm_rope_kernel,
        out_shape=(jax.ShapeDtypeStruct((t, d), jnp.bfloat16), table, table),
        grid=(t // rows,),
        in_specs=[pl.BlockSpec((rows, d), row_blk), pl.BlockSpec((1, d), const), pl.BlockSpec((rows, 1), row_blk),
                  pl.BlockSpec((1, HEAD_DIM), const), pl.BlockSpec((1, HEAD_DIM), const)],
        out_specs=(pl.BlockSpec((rows, d), row_blk), pl.BlockSpec((rows, HEAD_DIM), row_blk),
                   pl.BlockSpec((rows, HEAD_DIM), row_blk)),
        compiler_params=_params("parallel"),
        name="rmsnorm_rope_tables",
    )(x, g.reshape(1, d), positions.reshape(t, 1), freq, sign)


def _rope(xb, cos_t, sin_t, rot):
    partner = jnp.dot(xb, rot, preferred_element_type=jnp.float32)
    return xb.astype(jnp.float32) * cos_t + partner * sin_t


def _swa_kernel(sink_ref, q_ref, kc_ref, vc_ref, kp_ref, vp_ref, cosc_ref, sinc_ref, cosp_ref, sinp_ref, rot_ref,
                o_ref):
    m_step = pl.program_id(1)
    blk = WINDOW
    cols = GQA_GROUP * blk
    rot = rot_ref[...]
    cos_b = [cosp_ref[0]] + [cosc_ref[0, t * blk:(t + 1) * blk] for t in range(SWA_BLOCKS)]
    sin_b = [sinp_ref[0]] + [sinc_ref[0, t * blk:(t + 1) * blk] for t in range(SWA_BLOCKS)]

    kj = lax.broadcasted_iota(jnp.int32, (blk, cols), 0)
    col = lax.broadcasted_iota(jnp.int32, (blk, cols), 1)
    cur_live = kj <= col % blk
    first_live = jnp.logical_or(cur_live, m_step > 0)
    col_group = lax.broadcasted_iota(jnp.int32, (1, cols), 1) // blk

    for h in range(N_KV_HEADS):
        hs = slice(h * HEAD_DIM, (h + 1) * HEAD_DIM)
        k_b = [_rope(kp_ref[0, :, hs], cos_b[0], sin_b[0], rot).astype(jnp.bfloat16)]
        v_b = [vp_ref[0, :, hs]]
        for t in range(SWA_BLOCKS):
            rows = slice(t * blk, (t + 1) * blk)
            k_b.append(_rope(kc_ref[0, rows, hs], cos_b[t + 1], sin_b[t + 1], rot).astype(jnp.bfloat16))
            v_b.append(vc_ref[0, rows, hs])
        sink = jnp.zeros((1, cols), jnp.float32)
        for g in range(GQA_GROUP):
            sink = jnp.where(col_group == g, sink_ref[h * GQA_GROUP + g], sink)
        for t in range(SWA_BLOCKS):
            rows = slice(t * blk, (t + 1) * blk)
            k = jnp.concatenate([k_b[t], k_b[t + 1]], axis=0)
            v = jnp.concatenate([v_b[t], v_b[t + 1]], axis=0)
            q_parts = []
            for g in range(GQA_GROUP):
                c0 = (h * GQA_GROUP + g) * HEAD_DIM
                q_parts.append(_rope(q_ref[0, rows, c0:c0 + HEAD_DIM], cos_b[t + 1], sin_b[t + 1], rot))
            q = jnp.concatenate(q_parts, axis=0).astype(jnp.bfloat16)
            st = lax.dot_general(k, q, (((1,), (1,)), ((), ())), preferred_element_type=jnp.float32)
            s = jnp.where(cur_live, st[blk:], st[:blk]) * (HEAD_DIM ** -0.5)
            if t == 0:
                s = jnp.where(first_live, s, NEG)
            m = jnp.maximum(jnp.max(s, axis=0, keepdims=True), sink)
            e = jnp.exp(s - m)
            l = jnp.sum(e, axis=0, keepdims=True) + jnp.exp(sink - m)
            p = e * (1.0 / l)
            pt = jnp.concatenate([jnp.where(cur_live, 0.0, p), jnp.where(cur_live, p, 0.0)], axis=0)
            o = lax.dot_general(pt.astype(jnp.bfloat16), v, (((0,), (0,)), ((), ())),
                                preferred_element_type=jnp.float32)
            for g in range(GQA_GROUP):
                c0 = (h * GQA_GROUP + g) * HEAD_DIM
                o_ref[0, rows, c0:c0 + HEAD_DIM] = o[g * blk:(g + 1) * blk].astype(o_ref.dtype)


def swa_attention(q, kvx, sink, cos_t, sin_t):
    b, s, _ = q.shape
    blk = WINDOW
    tile = SWA_BLOCKS * blk
    half = ROT_DIM // 2
    src_lane = lax.broadcasted_iota(jnp.int32, (HEAD_DIM, HEAD_DIM), 0)
    dst_lane = lax.broadcasted_iota(jnp.int32, (HEAD_DIM, HEAD_DIM), 1)
    partner_of_dst = jnp.where(dst_lane < half, dst_lane + half, dst_lane - half)
    rot = jnp.logical_and(dst_lane < ROT_DIM, src_lane == partner_of_dst).astype(jnp.bfloat16)
    cur = lambda bi, m, sk: (bi, m, 0)
    prev = lambda bi, m, sk: (bi, jnp.maximum(m * SWA_BLOCKS - 1, 0), 0)
    grid_spec = pltpu.PrefetchScalarGridSpec(
        num_scalar_prefetch=1,
        grid=(b, s // tile),
        in_specs=[
            pl.BlockSpec((1, tile, SELF_W), cur),
            pl.BlockSpec((1, tile, KV_W), cur),
            pl.BlockSpec((1, tile, KV_W), lambda bi, m, sk: (bi, m, 1)),
            pl.BlockSpec((1, blk, KV_W), prev),
            pl.BlockSpec((1, blk, KV_W), lambda bi, m, sk: (bi, jnp.maximum(m * SWA_BLOCKS - 1, 0), 1)),
            pl.BlockSpec((1, tile, HEAD_DIM), cur),
            pl.BlockSpec((1, tile, HEAD_DIM), cur),
            pl.BlockSpec((1, blk, HEAD_DIM), prev),
            pl.BlockSpec((1, blk, HEAD_DIM), prev),
            pl.BlockSpec((HEAD_DIM, HEAD_DIM), lambda bi, m, sk: (0, 0)),
        ],
        out_specs=pl.BlockSpec((1, tile, SELF_W), cur),
    )
    return pl.pallas_call(
        _swa_kernel,
        out_shape=jax.ShapeDtypeStruct((b, s, SELF_W), jnp.bfloat16),
        grid_spec=grid_spec,
        compiler_params=_params("parallel", "parallel"),
        name="swa_attention",
    )(sink, q, kvx, kvx, kvx, kvx, cos_t, sin_t, cos_t, sin_t, rot)


def kernel(x, mem, positions, norm_mix, norm_mem, norm_mlp, w_mem_kv, pool_w_in, pool_w_group, pool_scale,
           pool_w_out, attn_w_in, attn_sink, attn_w_out, mlp_w1, mlp_w2, final_norm):
    b, s, d = x.shape
    t = b * s
    bf16 = jnp.bfloat16
    h = x.reshape(t, d)
    hn, cos_t, sin_t = rmsnorm_and_rope_tables(h, norm_mix[0], positions.reshape(t))
    cos_t, sin_t = cos_t.reshape(b, s, HEAD_DIM), sin_t.reshape(b, s, HEAD_DIM)
    mem_kv_all = mem_projections(mem.reshape(b * MEM_LEN, d), norm_mem, w_mem_kv)
    hb = pss = None
    for i in range(2):
        mem_kv = mem_kv_all[i].reshape(b, MEM_LEN, 2 * XA_W)
        if i == 0:
            proj = matmul_wres([hn], pool_w_in, 0, SELF_W + XA_W).reshape(b, s, -1)
            y_self = pool_mixer(proj, pool_w_group[0].astype(bf16), pool_scale[0])
            y_mem = mem_xattn(proj, SELF_W, mem_kv)
            w_out = pool_w_out
        else:
            norm = (norm_mix[i], pss)
            q = matmul_wres([hb], attn_w_in, 0, SELF_W, norm=norm).reshape(b, s, -1)
            kvx = matmul_wres([hb], attn_w_in, 0, 2 * KV_W + XA_W, bn=2 * KV_W + XA_W, col0=SELF_W, norm=norm)
            kvx = kvx.reshape(b, s, -1)
            y_self = swa_attention(q, kvx, attn_sink[0], cos_t, sin_t)
            y_mem = mem_xattn(kvx, 2 * KV_W, mem_kv)
            w_out = attn_w_out
        h, hb, pss = matmul_wres([y_self.reshape(t, SELF_W), y_mem.reshape(t, XA_W)], w_out, 0, d,
                                 epilogue="residual", res=h, out_dtype=jnp.float32, emit_norm=True,
                                 vmem_limit_bytes=VMEM_LIMIT_BYTES_LARGE)
        a, w2b = matmul_wres([hb], mlp_w1, i, mlp_w1.shape[2], epilogue="relu2", norm=(norm_mlp[i], pss),
                             side=(mlp_w2, i))
        if i == 0:
            h, hb, pss = matmul_kacc(a, w2b, h, emit_norm=True)
        else:
            h = matmul_kacc(a, w2b, h)
    return rmsnorm(h, final_norm, jnp.float32).reshape(b, s, d)
```

```python
import functools

import jax
import jax.numpy as jnp
from jax import lax
from jax.experimental import pallas as pl
from jax.experimental.pallas import tpu as pltpu

D_MODEL = 4096
SELF_W = 3072
XA_W = 1024
XA_HEADS = 4
XA_HEAD_DIM = 256
MEM_LEN = 256
POOL_WINDOWS = (2, 4, 8, 16)
POOL_GROUP = 768
HEAD_DIM = 128
N_Q_HEADS = 24
GQA_GROUP = 8
N_KV_HEADS = 3
KV_W = 384
WINDOW = 128
ROT_DIM = 32
ROPE_THETA = 500000.0
EPS = 1e-6
NEG = -1e30

VMEM_LIMIT_BYTES = 56 * 1024 * 1024
VMEM_LIMIT_BYTES_LARGE = 60 * 1024 * 1024
LANES = 128
MM_ROW_CHUNK = 512
SWA_BLOCKS = 8
POOL_HALO = 16


def _params(*sem, vmem_limit_bytes=VMEM_LIMIT_BYTES):
    return pltpu.CompilerParams(dimension_semantics=sem, vmem_limit_bytes=vmem_limit_bytes)


def _rmsnorm_kernel(x_ref, g_ref, o_ref):
    x = x_ref[...]
    ms = jnp.mean(x * x, axis=-1, keepdims=True)
    o_ref[...] = (x * lax.rsqrt(ms + EPS) * g_ref[...]).astype(o_ref.dtype)


def rmsnorm(x, g, out_dtype, rows=512):
    r, d = x.shape
    return pl.pallas_call(
        _rmsnorm_kernel,
        out_shape=jax.ShapeDtypeStruct((r, d), out_dtype),
        grid=(r // rows,),
        in_specs=[pl.BlockSpec((rows, d), lambda i: (i, 0)),
                  pl.BlockSpec((1, d), lambda i: (0, 0))],
        out_specs=pl.BlockSpec((rows, d), lambda i: (i, 0)),
        compiler_params=_params("parallel"),
        name="rmsnorm",
    )(x, g.reshape(1, d))


def _epilogue(acc, epilogue, res_ref, o_ref, rs):
    if epilogue == "relu2":
        acc = jnp.square(jnp.maximum(acc, 0.0))
    elif epilogue == "residual":
        acc = res_ref[rs, :] + acc
    o_ref[rs, :] = acc.astype(o_ref.dtype)
    return acc


def _lane_group_sumsq(h_new):
    sq = h_new * h_new
    pss = sq[:, :LANES]
    for c in range(1, sq.shape[1] // LANES):
        pss = pss + sq[:, c * LANES:(c + 1) * LANES]
    return pss


def _row_rstd(pss_ref, d):
    return lax.rsqrt(jnp.sum(pss_ref[...], axis=-1, keepdims=True) * (1.0 / d) + EPS)


def _mm_wres_kernel(*refs, n_a, k_splits, chunk, n_j, row_chunk, epilogue, normed, emit_norm, side):
    refs = list(refs)
    a_refs = [refs.pop(0) for _ in range(n_a)]
    w_ref = refs.pop(0)
    pss_in_ref = refs.pop(0) if normed else None
    res_ref = refs.pop(0) if epilogue == "residual" else None
    next_gain_ref = refs.pop(0) if emit_norm else None
    side_in_ref = refs.pop(0) if side else None
    o_ref = refs.pop(0)
    hb_ref = refs.pop(0) if emit_norm else None
    pss_out_ref = refs.pop(0) if emit_norm else None
    side_out_ref = refs.pop(0) if side else None
    (wbf_ref,) = refs
    jp = pl.program_id(0)
    i = pl.program_id(1)
    n_slots = wbf_ref.shape[0]

    @pl.when(jp < n_j)
    def _():
        row0 = pl.multiple_of(i * chunk, chunk)
        wbf_ref[jp % n_slots, pl.ds(row0, chunk), :] = w_ref[0].astype(jnp.bfloat16)

    @pl.when(jp > 0)
    def _():
        use_slot = (jp - 1) % n_slots
        rstd = _row_rstd(pss_in_ref, wbf_ref.shape[1]) if normed else None
        for r0 in range(0, o_ref.shape[0], row_chunk):
            rs = slice(r0, r0 + row_chunk)
            acc = None
            off = 0
            for a_ref, kw in zip(a_refs, k_splits):
                part = jnp.dot(a_ref[rs, :], wbf_ref[use_slot, pl.ds(off, kw), :], preferred_element_type=jnp.float32)
                acc = part if acc is None else acc + part
                off += kw
            if normed:
                acc = acc * rstd[rs]
            total = _epilogue(acc, epilogue, res_ref, o_ref, rs)
            if emit_norm:
                hb_ref[rs, :] = (total * next_gain_ref[...]).astype(hb_ref.dtype)
                pss_out_ref[rs, :] = _lane_group_sumsq(total)
        if side:
            side_out_ref[...] = side_in_ref[...].astype(side_out_ref.dtype)


def matmul_wres(a_list, w, layer, n_out, *, epilogue="cast", res=None, out_dtype=jnp.bfloat16, bm=1024, bn=1024,
                col0=0, row_sumsq=None, next_gain=None, side=None, vmem_limit_bytes=VMEM_LIMIT_BYTES):
    emit_norm = next_gain is not None
    m = a_list[0].shape[0]
    k_total = w.shape[1]
    k_splits = tuple(a.shape[1] for a in a_list)
    n_i, n_j = m // bm, n_out // bn
    chunk = k_total // n_i
    assert sum(k_splits) == k_total and n_i * bm == m and n_j * bn == n_out and chunk * n_i == k_total
    assert chunk % 16 == 0 and (not emit_norm or epilogue == "residual")

    def row_blk(jp, i):
        return jnp.where(jp == 0, 0, i)

    def stage_blk(jp, i):
        return jnp.where(jp < n_j, i, 0)

    o_map = lambda jp, i: (row_blk(jp, i), jnp.maximum(jp - 1, 0))
    in_specs = [pl.BlockSpec((bm, kw), lambda jp, i: (row_blk(jp, i), 0)) for kw in k_splits]
    in_specs.append(pl.BlockSpec((pl.Element(1), pl.Element(chunk), pl.Element(bn)),
                                 lambda jp, i: (layer, pl.multiple_of(stage_blk(jp, i) * chunk, chunk),
                                                pl.multiple_of(col0 + jnp.minimum(jp, n_j - 1) * bn, LANES))))
    args = list(a_list) + [w]
    if row_sumsq is not None:
        in_specs.append(pl.BlockSpec((bm, row_sumsq.shape[1]), lambda jp, i: (row_blk(jp, i), 0)))
        args.append(row_sumsq)
    if epilogue == "residual":
        in_specs.append(pl.BlockSpec((bm, bn), o_map))
        args.append(res)
    if emit_norm:
        in_specs.append(pl.BlockSpec((1, bn), lambda jp, i: (0, jnp.maximum(jp - 1, 0))))
        args.append(next_gain.reshape(1, n_out))
    out_shape = [jax.ShapeDtypeStruct((m, n_out), out_dtype)]
    out_specs = [pl.BlockSpec((bm, bn), o_map)]
    if emit_norm:
        out_shape += [jax.ShapeDtypeStruct((m, n_out), jnp.bfloat16),
                      jax.ShapeDtypeStruct((m, n_j * LANES), jnp.float32)]
        out_specs += [pl.BlockSpec((bm, bn), o_map), pl.BlockSpec((bm, LANES), o_map)]
    if side is not None:
        w_side, layer2 = side
        _, k2, n2 = w_side.shape
        rows2 = k2 // (n_j * n_i)
        assert rows2 * n_j * n_i == k2 and rows2 % 16 == 0
        slab = lambda jp, i: jnp.where(jp == 0, 0, (jp - 1) * n_i + i)
        in_specs.append(pl.BlockSpec((None, rows2, n2), lambda jp, i: (layer2, slab(jp, i), 0)))
        args.append(w_side)
        out_shape.append(jax.ShapeDtypeStruct((k2, n2), jnp.bfloat16))
        out_specs.append(pl.BlockSpec((rows2, n2), lambda jp, i: (slab(jp, i), 0)))
    outs = pl.pallas_call(
        functools.partial(_mm_wres_kernel, n_a=len(a_list), k_splits=k_splits, chunk=chunk, n_j=n_j,
                          row_chunk=min(bm, MM_ROW_CHUNK), epilogue=epilogue,
                          normed=row_sumsq is not None, emit_norm=emit_norm, side=side is not None),
        out_shape=out_shape,
        grid=(n_j + 1, n_i),
        in_specs=in_specs,
        out_specs=out_specs,
        scratch_shapes=[pltpu.VMEM((min(2, n_j), k_total, bn), jnp.bfloat16)],
        compiler_params=_params("arbitrary", "arbitrary", vmem_limit_bytes=vmem_limit_bytes),
        name="mm_wres_" + epilogue,
    )(*args)
    return outs[0] if len(outs) == 1 else tuple(outs)


def _mm_kacc_kernel(a_ref, w_ref, res_ref, *refs, nk):
    emit_norm = len(refs) > 1
    if emit_norm:
        next_gain_ref, o_ref, hb_ref, pss_ref = refs
    else:
        (o_ref,) = refs
    k = pl.program_id(2)

    @pl.when(k == 0)
    def _():
        o_ref[...] = res_ref[...]

    row_slices = [slice(r0, r0 + MM_ROW_CHUNK) for r0 in range(0, o_ref.shape[0], MM_ROW_CHUNK)]

    def partial_product(rs):
        return jnp.dot(a_ref[rs, :], w_ref[...], preferred_element_type=jnp.float32)

    def accumulate():
        for rs in row_slices:
            o_ref[rs, :] += partial_product(rs)

    if not emit_norm:
        accumulate()
    else:
        pl.when(k < nk - 1)(accumulate)

        @pl.when(k == nk - 1)
        def _():
            for rs in row_slices:
                total = o_ref[rs, :] + partial_product(rs)
                o_ref[rs, :] = total
                hb_ref[rs, :] = (total * next_gain_ref[...]).astype(hb_ref.dtype)
                pss_ref[rs, :] = _lane_group_sumsq(total)


def matmul_kacc(a, w, res, *, next_gain=None, bm=1024, bn=1024, bk=4096):
    m, k_total = a.shape
    n = w.shape[1]
    assert m % bm == 0 and n % bn == 0 and k_total % bk == 0
    o_map = lambda j, i, k: (i, j)
    in_specs = [pl.BlockSpec((bm, bk), lambda j, i, k: (i, k)),
                pl.BlockSpec((bk, bn), lambda j, i, k: (k, j)),
                pl.BlockSpec((bm, bn), o_map)]
    args = [a, w, res]
    out_shape = [jax.ShapeDtypeStruct((m, n), jnp.float32)]
    out_specs = [pl.BlockSpec((bm, bn), o_map)]
    if next_gain is not None:
        in_specs.append(pl.BlockSpec((1, bn), lambda j, i, k: (0, j)))
        args.append(next_gain.reshape(1, n))
        out_shape += [jax.ShapeDtypeStruct((m, n), jnp.bfloat16),
                      jax.ShapeDtypeStruct((m, (n // bn) * LANES), jnp.float32)]
        out_specs += [pl.BlockSpec((bm, bn), o_map), pl.BlockSpec((bm, LANES), o_map)]
    outs = pl.pallas_call(
        functools.partial(_mm_kacc_kernel, nk=k_total // bk),
        out_shape=out_shape,
        grid=(n // bn, m // bm, k_total // bk),
        in_specs=in_specs,
        out_specs=out_specs,
        compiler_params=_params("parallel", "parallel", "arbitrary", vmem_limit_bytes=VMEM_LIMIT_BYTES_LARGE),
        name="mm_kacc_residual",
    )(*args)
    return outs[0] if len(outs) == 1 else tuple(outs)


def _mem_proj_kernel(x_ref, g_ref, w_ref, o_ref, xn_ref):
    @pl.when(pl.program_id(1) == 0)
    def _():
        x = x_ref[...]
        ms = jnp.mean(x * x, axis=-1, keepdims=True)
        xn_ref[...] = (x * lax.rsqrt(ms + EPS) * g_ref[...]).astype(xn_ref.dtype)

    o_ref[...] = jnp.dot(xn_ref[...], w_ref[...].astype(jnp.bfloat16),
                         preferred_element_type=jnp.float32).astype(o_ref.dtype)


def mem_projections(x, g, w, *, bn=512):
    m, k_total = x.shape
    n_layers, _, n = w.shape
    assert n % bn == 0
    return pl.pallas_call(
        _mem_proj_kernel,
        out_shape=jax.ShapeDtypeStruct((n_layers, m, n), jnp.bfloat16),
        grid=(n_layers, n // bn),
        in_specs=[pl.BlockSpec((m, k_total), lambda l, j: (0, 0)),
                  pl.BlockSpec((None, 1, k_total), lambda l, j: (l, 0, 0)),
                  pl.BlockSpec((None, k_total, bn), lambda l, j: (l, 0, j))],
        out_specs=pl.BlockSpec((None, m, bn), lambda l, j: (l, 0, j)),
        scratch_shapes=[pltpu.VMEM((m, k_total), jnp.bfloat16)],
        compiler_params=_params("arbitrary", "arbitrary"),
        name="mem_proj",
    )(x, g.reshape(n_layers, 1, k_total), w)


def _pool_kernel(u_ref, halo_ref, w_ref, scale_ref, o_ref, *, ts):
    i = pl.program_id(1)
    cur = u_ref[0].astype(jnp.float32)
    halo = halo_ref[0].astype(jnp.float32)
    halo = jnp.where(i > 0, halo, 0.0)
    t1 = (i * ts + 1 + lax.broadcasted_iota(jnp.int32, (ts, 1), 0)).astype(jnp.float32)
    for g, win in enumerate(POOL_WINDOWS):
        c0, c1 = g * POOL_GROUP, (g + 1) * POOL_GROUP
        x = jnp.concatenate([halo[:, c0:c1], cur[:, c0:c1]], axis=0)
        s = x
        d = 1
        while d < win:
            s = s[d:] + s[:-d]
            d *= 2
        s = s[POOL_HALO - (win - 1):]
        mean = s / jnp.minimum(t1, float(win))
        p = (mean - cur[:, c0:c1]).astype(jnp.bfloat16)
        y = jnp.dot(p, w_ref[g], preferred_element_type=jnp.float32)
        o_ref[0, :, c0:c1] = (y * scale_ref[:, c0:c1]).astype(o_ref.dtype)


def pool_mixer(proj, w_group, scale, ts=512):
    b, s, _ = proj.shape
    hb = ts // POOL_HALO
    return pl.pallas_call(
        functools.partial(_pool_kernel, ts=ts),
        out_shape=jax.ShapeDtypeStruct((b, s, SELF_W), jnp.bfloat16),
        grid=(b, s // ts),
        in_specs=[
            pl.BlockSpec((1, ts, SELF_W), lambda bi, i: (bi, i, 0)),
            pl.BlockSpec((1, POOL_HALO, SELF_W), lambda bi, i: (bi, jnp.maximum(i * hb - 1, 0), 0)),
            pl.BlockSpec((len(POOL_WINDOWS), POOL_GROUP, POOL_GROUP), lambda bi, i: (0, 0, 0)),
            pl.BlockSpec((1, SELF_W), lambda bi, i: (0, 0)),
        ],
        out_specs=pl.BlockSpec((1, ts, SELF_W), lambda bi, i: (bi, i, 0)),
        compiler_params=_params("parallel", "parallel"),
        name="pool_mixer",
    )(proj, proj, w_group, scale.reshape(1, SELF_W))


def _xattn_kernel(*refs):
    q_refs, kv_ref, o_ref = refs[:XA_HEADS], refs[XA_HEADS], refs[XA_HEADS + 1]
    for h in range(XA_HEADS):
        hs = slice(h * XA_HEAD_DIM, (h + 1) * XA_HEAD_DIM)
        q = q_refs[h][0]
        k = kv_ref[0, :, hs]
        v = kv_ref[0, :, XA_W + h * XA_HEAD_DIM:XA_W + (h + 1) * XA_HEAD_DIM]
        s = lax.dot_general(q, k, (((1,), (1,)), ((), ())), preferred_element_type=jnp.float32)
        s = s * (XA_HEAD_DIM ** -0.5)
        m = jnp.max(s, axis=-1, keepdims=True)
        e = jnp.exp(s - m)
        l = jnp.sum(e, axis=-1, keepdims=True)
        o = jnp.dot(e.astype(jnp.bfloat16), v, preferred_element_type=jnp.float32)
        o_ref[0, :, hs] = (o / l).astype(o_ref.dtype)


def mem_xattn(proj, q_col0, mem_kv, ts=1024):
    b, s, _ = proj.shape
    qb = q_col0 // XA_HEAD_DIM
    assert qb * XA_HEAD_DIM == q_col0
    q_specs = [pl.BlockSpec((1, ts, XA_HEAD_DIM), functools.partial(lambda bi, i, h: (bi, i, qb + h), h=h))
               for h in range(XA_HEADS)]
    return pl.pallas_call(
        _xattn_kernel,
        out_shape=jax.ShapeDtypeStruct((b, s, XA_W), jnp.bfloat16),
        grid=(b, s // ts),
        in_specs=q_specs + [pl.BlockSpec((1, MEM_LEN, 2 * XA_W), lambda bi, i: (bi, 0, 0))],
        out_specs=pl.BlockSpec((1, ts, XA_W), lambda bi, i: (bi, i, 0)),
        compiler_params=_params("parallel", "parallel"),
        name="mem_xattn",
    )(*([proj] * XA_HEADS), mem_kv)


def _norm_rope_kernel(x_ref, g_ref, pos_ref, freq_ref, sign_ref, o_ref, cos_ref, sin_ref):
    _rmsnorm_kernel(x_ref, g_ref, o_ref)
    ang = pos_ref[...].astype(jnp.float32) * freq_ref[...]
    cos_ref[...] = jnp.cos(ang)
    sin_ref[...] = jnp.sin(ang) * sign_ref[...]


def rmsnorm_and_rope_tables(x, g, positions, rows=512):
    t, d = x.shape
    half = ROT_DIM // 2
    inv_freq = ROPE_THETA ** (-jnp.arange(0, ROT_DIM, 2, dtype=jnp.float32) / ROT_DIM)
    freq = jnp.concatenate([inv_freq, inv_freq, jnp.zeros((HEAD_DIM - ROT_DIM,), jnp.float32)]).reshape(1, HEAD_DIM)
    sign = jnp.concatenate([-jnp.ones((half,), jnp.float32), jnp.ones((half,), jnp.float32),
                            jnp.zeros((HEAD_DIM - ROT_DIM,), jnp.float32)]).reshape(1, HEAD_DIM)
    table = jax.ShapeDtypeStruct((t, HEAD_DIM), jnp.float32)
    row_blk = lambda i: (i, 0)
    const = lambda i: (0, 0)
    return pl.pallas_call(
        _norm_rope_kernel,
        out_shape=(jax.ShapeDtypeStruct((t, d), jnp.bfloat16), table, table),
        grid=(t // rows,),
        in_specs=[pl.BlockSpec((rows, d), row_blk), pl.BlockSpec((1, d), const), pl.BlockSpec((rows, 1), row_blk),
                  pl.BlockSpec((1, HEAD_DIM), const), pl.BlockSpec((1, HEAD_DIM), const)],
        out_specs=(pl.BlockSpec((rows, d), row_blk), pl.BlockSpec((rows, HEAD_DIM), row_blk),
                   pl.BlockSpec((rows, HEAD_DIM), row_blk)),
        compiler_params=_params("parallel"),
        name="rmsnorm_rope_tables",
    )(x, g.reshape(1, d), positions.reshape(t, 1), freq, sign)


def _rope(xb, cos_t, sin_t, rot):
    partner = jnp.dot(xb, rot, preferred_element_type=jnp.float32)
    return xb.astype(jnp.float32) * cos_t + partner * sin_t


def _swa_kernel(sink_ref, q_ref, kc_ref, vc_ref, kp_ref, vp_ref, cosc_ref, sinc_ref, cosp_ref, sinp_ref, rot_ref,
                o_ref):
    m_step = pl.program_id(1)
    blk = WINDOW
    cols = GQA_GROUP * blk
    rot = rot_ref[...]
    cos_b = [cosp_ref[0]] + [cosc_ref[0, t * blk:(t + 1) * blk] for t in range(SWA_BLOCKS)]
    sin_b = [sinp_ref[0]] + [sinc_ref[0, t * blk:(t + 1) * blk] for t in range(SWA_BLOCKS)]

    kj = lax.broadcasted_iota(jnp.int32, (blk, cols), 0)
    col = lax.broadcasted_iota(jnp.int32, (blk, cols), 1)
    cur_live = kj <= col % blk
    first_live = jnp.logical_or(cur_live, m_step > 0)
    col_group = lax.broadcasted_iota(jnp.int32, (1, cols), 1) // blk

    for h in range(N_KV_HEADS):
        hs = slice(h * HEAD_DIM, (h + 1) * HEAD_DIM)
        k_b = [_rope(kp_ref[0, :, hs], cos_b[0], sin_b[0], rot).astype(jnp.bfloat16)]
        v_b = [vp_ref[0, :, hs]]
        for t in range(SWA_BLOCKS):
            rows = slice(t * blk, (t + 1) * blk)
            k_b.append(_rope(kc_ref[0, rows, hs], cos_b[t + 1], sin_b[t + 1], rot).astype(jnp.bfloat16))
            v_b.append(vc_ref[0, rows, hs])
        sink = jnp.zeros((1, cols), jnp.float32)
        for g in range(GQA_GROUP):
            sink = jnp.where(col_group == g, sink_ref[h * GQA_GROUP + g], sink)
        for t in range(SWA_BLOCKS):
            rows = slice(t * blk, (t + 1) * blk)
            k = jnp.concatenate([k_b[t], k_b[t + 1]], axis=0)
            v = jnp.concatenate([v_b[t], v_b[t + 1]], axis=0)
            q_parts = []
            for g in range(GQA_GROUP):
                c0 = (h * GQA_GROUP + g) * HEAD_DIM
                q_parts.append(_rope(q_ref[0, rows, c0:c0 + HEAD_DIM], cos_b[t + 1], sin_b[t + 1], rot))
            q = jnp.concatenate(q_parts, axis=0).astype(jnp.bfloat16)
            st = lax.dot_general(k, q, (((1,), (1,)), ((), ())), preferred_element_type=jnp.float32)
            s = jnp.where(cur_live, st[blk:], st[:blk]) * (HEAD_DIM ** -0.5)
            if t == 0:
                s = jnp.where(first_live, s, NEG)
            m = jnp.maximum(jnp.max(s, axis=0, keepdims=True), sink)
            e = jnp.exp(s - m)
            l = jnp.sum(e, axis=0, keepdims=True) + jnp.exp(sink - m)
            p = e * (1.0 / l)
            pt = jnp.concatenate([jnp.where(cur_live, 0.0, p), jnp.where(cur_live, p, 0.0)], axis=0)
            o = lax.dot_general(pt.astype(jnp.bfloat16), v, (((0,), (0,)), ((), ())),
                                preferred_element_type=jnp.float32)
            for g in range(GQA_GROUP):
                c0 = (h * GQA_GROUP + g) * HEAD_DIM
                o_ref[0, rows, c0:c0 + HEAD_DIM] = o[g * blk:(g + 1) * blk].astype(o_ref.dtype)


def swa_attention(q, kvx, sink, cos_t, sin_t):
    b, s, _ = q.shape
    blk = WINDOW
    tile = SWA_BLOCKS * blk
    half = ROT_DIM // 2
    src_lane = lax.broadcasted_iota(jnp.int32, (HEAD_DIM, HEAD_DIM), 0)
    dst_lane = lax.broadcasted_iota(jnp.int32, (HEAD_DIM, HEAD_DIM), 1)
    partner_of_dst = jnp.where(dst_lane < half, dst_lane + half, dst_lane - half)
    rot = jnp.logical_and(dst_lane < ROT_DIM, src_lane == partner_of_dst).astype(jnp.bfloat16)
    cur = lambda bi, m, sk: (bi, m, 0)
    prev = lambda bi, m, sk: (bi, jnp.maximum(m * SWA_BLOCKS - 1, 0), 0)
    grid_spec = pltpu.PrefetchScalarGridSpec(
        num_scalar_prefetch=1,
        grid=(b, s // tile),
        in_specs=[
            pl.BlockSpec((1, tile, SELF_W), cur),
            pl.BlockSpec((1, tile, KV_W), cur),
            pl.BlockSpec((1, tile, KV_W), lambda bi, m, sk: (bi, m, 1)),
            pl.BlockSpec((1, blk, KV_W), prev),
            pl.BlockSpec((1, blk, KV_W), lambda bi, m, sk: (bi, jnp.maximum(m * SWA_BLOCKS - 1, 0), 1)),
            pl.BlockSpec((1, tile, HEAD_DIM), cur),
            pl.BlockSpec((1, tile, HEAD_DIM), cur),
            pl.BlockSpec((1, blk, HEAD_DIM), prev),
            pl.BlockSpec((1, blk, HEAD_DIM), prev),
            pl.BlockSpec((HEAD_DIM, HEAD_DIM), lambda bi, m, sk: (0, 0)),
        ],
        out_specs=pl.BlockSpec((1, tile, SELF_W), cur),
    )
    return pl.pallas_call(
        _swa_kernel,
        out_shape=jax.ShapeDtypeStruct((b, s, SELF_W), jnp.bfloat16),
        grid_spec=grid_spec,
        compiler_params=_params("parallel", "parallel"),
        name="swa_attention",
    )(sink, q, kvx, kvx, kvx, kvx, cos_t, sin_t, cos_t, sin_t, rot)


def kernel(x, mem, positions, norm_mix, norm_mem, norm_mlp, w_mem_kv, pool_w_in, pool_w_group, pool_scale,
           pool_w_out, attn_w_in, attn_sink, attn_w_out, mlp_w1, mlp_w2, final_norm):
    b, s, d = x.shape
    t = b * s
    bf16 = jnp.bfloat16
    h = x.reshape(t, d)
    hn, cos_t, sin_t = rmsnorm_and_rope_tables(h, norm_mix[0], positions.reshape(t))
    cos_t, sin_t = cos_t.reshape(b, s, HEAD_DIM), sin_t.reshape(b, s, HEAD_DIM)
    mem_kv_all = mem_projections(mem.reshape(b * MEM_LEN, d), norm_mem, w_mem_kv)
    hg = pss = None
    for i in range(2):
        mem_kv = mem_kv_all[i].reshape(b, MEM_LEN, 2 * XA_W)
        if i == 0:
            proj = matmul_wres([hn], pool_w_in, 0, SELF_W + XA_W).reshape(b, s, -1)
            y_self = pool_mixer(proj, pool_w_group[0].astype(bf16), pool_scale[0])
            y_mem = mem_xattn(proj, SELF_W, mem_kv)
            w_out = pool_w_out
        else:
            q = matmul_wres([hg], attn_w_in, 0, SELF_W, row_sumsq=pss).reshape(b, s, -1)
            kvx = matmul_wres([hg], attn_w_in, 0, 2 * KV_W + XA_W, bn=2 * KV_W + XA_W, col0=SELF_W, row_sumsq=pss)
            kvx = kvx.reshape(b, s, -1)
            y_self = swa_attention(q, kvx, attn_sink[0], cos_t, sin_t)
            y_mem = mem_xattn(kvx, 2 * KV_W, mem_kv)
            w_out = attn_w_out
        h, hg, pss = matmul_wres([y_self.reshape(t, SELF_W), y_mem.reshape(t, XA_W)], w_out, 0, d,
                                 epilogue="residual", res=h, out_dtype=jnp.float32, next_gain=norm_mlp[i],
                                 vmem_limit_bytes=VMEM_LIMIT_BYTES_LARGE)
        a, w2b = matmul_wres([hg], mlp_w1, i, mlp_w1.shape[2], epilogue="relu2", row_sumsq=pss, side=(mlp_w2, i))
        if i == 0:
            h, hg, pss = matmul_kacc(a, w2b, h, next_gain=norm_mix[1])
        else:
            h = matmul_kacc(a, w2b, h)
    return rmsnorm(h, final_norm, jnp.float32).reshape(b, s, d)
```

```python
import functools

import jax
import jax.numpy as jnp
from jax import lax
from jax.experimental import pallas as pl
from jax.experimental.pallas import tpu as pltpu

D_MODEL = 4096
SELF_W = 3072
XA_W = 1024
XA_HEADS = 4
XA_HEAD_DIM = 256
MEM_LEN = 256
POOL_WINDOWS = (2, 4, 8, 16)
POOL_GROUP = 768
HEAD_DIM = 128
N_Q_HEADS = 24
GQA_GROUP = 8
N_KV_HEADS = 3
KV_W = 384
WINDOW = 128
ROT_DIM = 32
ROPE_THETA = 500000.0
EPS = 1e-6
NEG = -1e30

VMEM_LIMIT_BYTES = 56 * 1024 * 1024
VMEM_LIMIT_BYTES_LARGE = 60 * 1024 * 1024
LANES = 128
MM_ROW_CHUNK = 512
SWA_BLOCKS = 8
POOL_HALO = 16


def _params(*sem, vmem_limit_bytes=VMEM_LIMIT_BYTES):
    return pltpu.CompilerParams(dimension_semantics=sem, vmem_limit_bytes=vmem_limit_bytes)


def _rmsnorm_kernel(x_ref, g_ref, o_ref):
    x = x_ref[...]
    ms = jnp.mean(x * x, axis=-1, keepdims=True)
    o_ref[...] = (x * lax.rsqrt(ms + EPS) * g_ref[...]).astype(o_ref.dtype)


def rmsnorm(x, g, out_dtype, rows=512):
    r, d = x.shape
    return pl.pallas_call(
        _rmsnorm_kernel,
        out_shape=jax.ShapeDtypeStruct((r, d), out_dtype),
        grid=(r // rows,),
        in_specs=[pl.BlockSpec((rows, d), lambda i: (i, 0)),
                  pl.BlockSpec((1, d), lambda i: (0, 0))],
        out_specs=pl.BlockSpec((rows, d), lambda i: (i, 0)),
        compiler_params=_params("parallel"),
        name="rmsnorm",
    )(x, g.reshape(1, d))


def _epilogue(acc, epilogue, res_ref, o_ref, rs):
    if epilogue == "relu2":
        acc = jnp.square(jnp.maximum(acc, 0.0))
    elif epilogue == "residual":
        acc = res_ref[rs, :] + acc
    o_ref[rs, :] = acc.astype(o_ref.dtype)
    return acc


def _lane_group_sumsq(h_new):
    sq = h_new * h_new
    pss = sq[:, :LANES]
    for c in range(1, sq.shape[1] // LANES):
        pss = pss + sq[:, c * LANES:(c + 1) * LANES]
    return pss


def _row_rstd(pss_ref, d):
    return lax.rsqrt(jnp.sum(pss_ref[...], axis=-1, keepdims=True) * (1.0 / d) + EPS)


def _mm_wres_kernel(*refs, n_a, k_splits, chunk, n_j, row_chunk, epilogue, normed, emit_norm, side):
    refs = list(refs)
    a_refs = [refs.pop(0) for _ in range(n_a)]
    w_ref = refs.pop(0)
    pss_in_ref = refs.pop(0) if normed else None
    res_ref = refs.pop(0) if epilogue == "residual" else None
    next_gain_ref = refs.pop(0) if emit_norm else None
    side_in_ref = refs.pop(0) if side else None
    o_ref = refs.pop(0)
    hb_ref = refs.pop(0) if emit_norm else None
    pss_out_ref = refs.pop(0) if emit_norm else None
    side_out_ref = refs.pop(0) if side else None
    (wbf_ref,) = refs
    jp = pl.program_id(0)
    i = pl.program_id(1)
    n_slots = wbf_ref.shape[0]

    @pl.when(jp < n_j)
    def _():
        row0 = pl.multiple_of(i * chunk, chunk)
        wbf_ref[jp % n_slots, pl.ds(row0, chunk), :] = w_ref[0].astype(jnp.bfloat16)

    @pl.when(jp > 0)
    def _():
        use_slot = (jp - 1) % n_slots
        rstd = _row_rstd(pss_in_ref, wbf_ref.shape[1]) if normed else None
        for r0 in range(0, o_ref.shape[0], row_chunk):
            rs = slice(r0, r0 + row_chunk)
            acc = None
            off = 0
            for a_ref, kw in zip(a_refs, k_splits):
                part = jnp.dot(a_ref[rs, :], wbf_ref[use_slot, pl.ds(off, kw), :], preferred_element_type=jnp.float32)
                acc = part if acc is None else acc + part
                off += kw
            if normed:
                acc = acc * rstd[rs]
            total = _epilogue(acc, epilogue, res_ref, o_ref, rs)
            if emit_norm:
                hb_ref[rs, :] = (total * next_gain_ref[...]).astype(hb_ref.dtype)
                pss_out_ref[rs, :] = _lane_group_sumsq(total)
        if side:
            side_out_ref[...] = side_in_ref[...].astype(side_out_ref.dtype)


def matmul_wres(a_list, w, layer, n_out, *, epilogue="cast", res=None, out_dtype=jnp.bfloat16, bm=1024, bn=1024,
                col0=0, row_sumsq=None, next_gain=None, side=None, vmem_limit_bytes=VMEM_LIMIT_BYTES):
    emit_norm = next_gain is not None
    m = a_list[0].shape[0]
    k_total = w.shape[1]
    k_splits = tuple(a.shape[1] for a in a_list)
    n_i, n_j = m // bm, n_out // bn
    chunk = k_total // n_i
    assert sum(k_splits) == k_total and n_i * bm == m and n_j * bn == n_out and chunk * n_i == k_total
    assert chunk % 16 == 0 and (not emit_norm or epilogue == "residual")

    def row_blk(jp, i):
        return jnp.where(jp == 0, 0, i)

    def stage_blk(jp, i):
        return jnp.where(jp < n_j, i, 0)

    o_map = lambda jp, i: (row_blk(jp, i), jnp.maximum(jp - 1, 0))
    in_specs = [pl.BlockSpec((bm, kw), lambda jp, i: (row_blk(jp, i), 0)) for kw in k_splits]
    in_specs.append(pl.BlockSpec((pl.Element(1), pl.Element(chunk), pl.Element(bn)),
                                 lambda jp, i: (layer, pl.multiple_of(stage_blk(jp, i) * chunk, chunk),
                                                pl.multiple_of(col0 + jnp.minimum(jp, n_j - 1) * bn, LANES))))
    args = list(a_list) + [w]
    if row_sumsq is not None:
        in_specs.append(pl.BlockSpec((bm, row_sumsq.shape[1]), lambda jp, i: (row_blk(jp, i), 0)))
        args.append(row_sumsq)
    if epilogue == "residual":
        in_specs.append(pl.BlockSpec((bm, bn), o_map))
        args.append(res)
    if emit_norm:
        in_specs.append(pl.BlockSpec((1, bn), lambda jp, i: (0, jnp.maximum(jp - 1, 0))))
        args.append(next_gain.reshape(1, n_out))
    out_shape = [jax.ShapeDtypeStruct((m, n_out), out_dtype)]
    out_specs = [pl.BlockSpec((bm, bn), o_map)]
    if emit_norm:
        out_shape += [jax.ShapeDtypeStruct((m, n_out), jnp.bfloat16),
                      jax.ShapeDtypeStruct((m, n_j * LANES), jnp.float32)]
        out_specs += [pl.BlockSpec((bm, bn), o_map), pl.BlockSpec((bm, LANES), o_map)]
    if side is not None:
        w_side, layer2 = side
        _, k2, n2 = w_side.shape
        rows2 = k2 // (n_j * n_i)
        assert rows2 * n_j * n_i == k2 and rows2 % 16 == 0
        slab = lambda jp, i: jnp.where(jp == 0, 0, (jp - 1) * n_i + i)
        in_specs.append(pl.BlockSpec((None, rows2, n2), lambda jp, i: (layer2, slab(jp, i), 0)))
        args.append(w_side)
        out_shape.append(jax.ShapeDtypeStruct((k2, n2), jnp.bfloat16))
        out_specs.append(pl.BlockSpec((rows2, n2), lambda jp, i: (slab(jp, i), 0)))
    outs = pl.pallas_call(
        functools.partial(_mm_wres_kernel, n_a=len(a_list), k_splits=k_splits, chunk=chunk, n_j=n_j,
                          row_chunk=min(bm, MM_ROW_CHUNK), epilogue=epilogue,
                          normed=row_sumsq is not None, emit_norm=emit_norm, side=side is not None),
        out_shape=out_shape,
        grid=(n_j + 1, n_i),
        in_specs=in_specs,
        out_specs=out_specs,
        scratch_shapes=[pltpu.VMEM((min(2, n_j), k_total, bn), jnp.bfloat16)],
        compiler_params=_params("arbitrary", "arbitrary", vmem_limit_bytes=vmem_limit_bytes),
        name="mm_wres_" + epilogue,
    )(*args)
    return outs[0] if len(outs) == 1 else tuple(outs)


def _mm_kacc_kernel(a_ref, w_ref, res_ref, *refs, nk):
    emit_norm = len(refs) > 1
    if emit_norm:
        next_gain_ref, o_ref, hb_ref, pss_ref = refs
    else:
        (o_ref,) = refs
    k = pl.program_id(2)
    row_slices = [slice(r0, r0 + MM_ROW_CHUNK) for r0 in range(0, o_ref.shape[0], MM_ROW_CHUNK)]

    def partial_product(rs):
        return jnp.dot(a_ref[rs, :], w_ref[...], preferred_element_type=jnp.float32)

    @pl.when(k == 0)
    def _():
        for rs in row_slices:
            o_ref[rs, :] = res_ref[rs, :] + partial_product(rs)

    @pl.when(jnp.logical_and(k > 0, k < nk - 1) if emit_norm else k > 0)
    def _():
        for rs in row_slices:
            o_ref[rs, :] += partial_product(rs)

    if emit_norm:
        @pl.when(k == nk - 1)
        def _():
            for rs in row_slices:
                total = o_ref[rs, :] + partial_product(rs)
                o_ref[rs, :] = total
                hb_ref[rs, :] = (total * next_gain_ref[...]).astype(hb_ref.dtype)
                pss_ref[rs, :] = _lane_group_sumsq(total)


def matmul_kacc(a, w, res, *, next_gain=None, bm=1024, bn=1024, bk=4096):
    m, k_total = a.shape
    n = w.shape[1]
    assert m % bm == 0 and n % bn == 0 and k_total % bk == 0 and k_total // bk >= 2
    o_map = lambda j, i, k: (i, j)
    in_specs = [pl.BlockSpec((bm, bk), lambda j, i, k: (i, k)),
                pl.BlockSpec((bk, bn), lambda j, i, k: (k, j)),
                pl.BlockSpec((bm, bn), o_map)]
    args = [a, w, res]
    out_shape = [jax.ShapeDtypeStruct((m, n), jnp.float32)]
    out_specs = [pl.BlockSpec((bm, bn), o_map)]
    if next_gain is not None:
        in_specs.append(pl.BlockSpec((1, bn), lambda j, i, k: (0, j)))
        args.append(next_gain.reshape(1, n))
        out_shape += [jax.ShapeDtypeStruct((m, n), jnp.bfloat16),
                      jax.ShapeDtypeStruct((m, (n // bn) * LANES), jnp.float32)]
        out_specs += [pl.BlockSpec((bm, bn), o_map), pl.BlockSpec((bm, LANES), o_map)]
    outs = pl.pallas_call(
        functools.partial(_mm_kacc_kernel, nk=k_total // bk),
        out_shape=out_shape,
        grid=(n // bn, m // bm, k_total // bk),
        in_specs=in_specs,
        out_specs=out_specs,
        compiler_params=_params("parallel", "parallel", "arbitrary", vmem_limit_bytes=VMEM_LIMIT_BYTES_LARGE),
        name="mm_kacc_residual",
    )(*args)
    return outs[0] if len(outs) == 1 else tuple(outs)


def _mem_proj_kernel(x_ref, g_ref, w_ref, o_ref, xn_ref):
    @pl.when(pl.program_id(1) == 0)
    def _():
        x = x_ref[...]
        ms = jnp.mean(x * x, axis=-1, keepdims=True)
        xn_ref[...] = (x * lax.rsqrt(ms + EPS) * g_ref[...]).astype(xn_ref.dtype)

    o_ref[...] = jnp.dot(xn_ref[...], w_ref[...].astype(jnp.bfloat16),
                         preferred_element_type=jnp.float32).astype(o_ref.dtype)


def mem_projections(x, g, w, *, bn=512):
    m, k_total = x.shape
    n_layers, _, n = w.shape
    assert n % bn == 0
    return pl.pallas_call(
        _mem_proj_kernel,
        out_shape=jax.ShapeDtypeStruct((n_layers, m, n), jnp.bfloat16),
        grid=(n_layers, n // bn),
        in_specs=[pl.BlockSpec((m, k_total), lambda l, j: (0, 0)),
                  pl.BlockSpec((None, 1, k_total), lambda l, j: (l, 0, 0)),
                  pl.BlockSpec((None, k_total, bn), lambda l, j: (l, 0, j))],
        out_specs=pl.BlockSpec((None, m, bn), lambda l, j: (l, 0, j)),
        scratch_shapes=[pltpu.VMEM((m, k_total), jnp.bfloat16)],
        compiler_params=_params("arbitrary", "arbitrary"),
        name="mem_proj",
    )(x, g.reshape(n_layers, 1, k_total), w)


def _pool_kernel(u_ref, halo_ref, w_ref, scale_ref, o_ref, *, ts):
    i = pl.program_id(1)
    cur = u_ref[0].astype(jnp.float32)
    halo = halo_ref[0].astype(jnp.float32)
    halo = jnp.where(i > 0, halo, 0.0)
    t1 = (i * ts + 1 + lax.broadcasted_iota(jnp.int32, (ts, 1), 0)).astype(jnp.float32)
    for g, win in enumerate(POOL_WINDOWS):
        c0, c1 = g * POOL_GROUP, (g + 1) * POOL_GROUP
        x = jnp.concatenate([halo[:, c0:c1], cur[:, c0:c1]], axis=0)
        s = x
        d = 1
        while d < win:
            s = s[d:] + s[:-d]
            d *= 2
        s = s[POOL_HALO - (win - 1):]
        mean = s / jnp.minimum(t1, float(win))
        p = (mean - cur[:, c0:c1]).astype(jnp.bfloat16)
        y = jnp.dot(p, w_ref[g], preferred_element_type=jnp.float32)
        o_ref[0, :, c0:c1] = (y * scale_ref[:, c0:c1]).astype(o_ref.dtype)


def pool_mixer(proj, w_group, scale, ts=512):
    b, s, _ = proj.shape
    hb = ts // POOL_HALO
    return pl.pallas_call(
        functools.partial(_pool_kernel, ts=ts),
        out_shape=jax.ShapeDtypeStruct((b, s, SELF_W), jnp.bfloat16),
        grid=(b, s // ts),
        in_specs=[
            pl.BlockSpec((1, ts, SELF_W), lambda bi, i: (bi, i, 0)),
            pl.BlockSpec((1, POOL_HALO, SELF_W), lambda bi, i: (bi, jnp.maximum(i * hb - 1, 0), 0)),
            pl.BlockSpec((len(POOL_WINDOWS), POOL_GROUP, POOL_GROUP), lambda bi, i: (0, 0, 0)),
            pl.BlockSpec((1, SELF_W), lambda bi, i: (0, 0)),
        ],
        out_specs=pl.BlockSpec((1, ts, SELF_W), lambda bi, i: (bi, i, 0)),
        compiler_params=_params("parallel", "parallel"),
        name="pool_mixer",
    )(proj, proj, w_group, scale.reshape(1, SELF_W))


def _xattn_kernel(*refs):
    q_refs, kv_ref, o_ref = refs[:XA_HEADS], refs[XA_HEADS], refs[XA_HEADS + 1]
    for h in range(XA_HEADS):
        hs = slice(h * XA_HEAD_DIM, (h + 1) * XA_HEAD_DIM)
        q = q_refs[h][0]
        k = kv_ref[0, :, hs]
        v = kv_ref[0, :, XA_W + h * XA_HEAD_DIM:XA_W + (h + 1) * XA_HEAD_DIM]
        s = lax.dot_general(q, k, (((1,), (1,)), ((), ())), preferred_element_type=jnp.float32)
        s = s * (XA_HEAD_DIM ** -0.5)
        m = jnp.max(s, axis=-1, keepdims=True)
        e = jnp.exp(s - m)
        l = jnp.sum(e, axis=-1, keepdims=True)
        o = jnp.dot(e.astype(jnp.bfloat16), v, preferred_element_type=jnp.float32)
        o_ref[0, :, hs] = (o / l).astype(o_ref.dtype)


def mem_xattn(proj, q_col0, mem_kv, ts=1024):
    b, s, _ = proj.shape
    qb = q_col0 // XA_HEAD_DIM
    assert qb * XA_HEAD_DIM == q_col0
    q_specs = [pl.BlockSpec((1, ts, XA_HEAD_DIM), functools.partial(lambda bi, i, h: (bi, i, qb + h), h=h))
               for h in range(XA_HEADS)]
    return pl.pallas_call(
        _xattn_kernel,
        out_shape=jax.ShapeDtypeStruct((b, s, XA_W), jnp.bfloat16),
        grid=(b, s // ts),
        in_specs=q_specs + [pl.BlockSpec((1, MEM_LEN, 2 * XA_W), lambda bi, i: (bi, 0, 0))],
        out_specs=pl.BlockSpec((1, ts, XA_W), lambda bi, i: (bi, i, 0)),
        compiler_params=_params("parallel", "parallel"),
        name="mem_xattn",
    )(*([proj] * XA_HEADS), mem_kv)


def _norm_rope_kernel(x_ref, g_ref, pos_ref, freq_ref, sign_ref, o_ref, cos_ref, sin_ref):
    _rmsnorm_kernel(x_ref, g_ref, o_ref)
    ang = pos_ref[...].astype(jnp.float32) * freq_ref[...]
    cos_ref[...] = jnp.cos(ang)
    sin_ref[...] = jnp.sin(ang) * sign_ref[...]


def rmsnorm_and_rope_tables(x, g, positions, rows=512):
    t, d = x.shape
    half = ROT_DIM // 2
    inv_freq = ROPE_THETA ** (-jnp.arange(0, ROT_DIM, 2, dtype=jnp.float32) / ROT_DIM)
    freq = jnp.concatenate([inv_freq, inv_freq, jnp.zeros((HEAD_DIM - ROT_DIM,), jnp.float32)]).reshape(1, HEAD_DIM)
    sign = jnp.concatenate([-jnp.ones((half,), jnp.float32), jnp.ones((half,), jnp.float32),
                            jnp.zeros((HEAD_DIM - ROT_DIM,), jnp.float32)]).reshape(1, HEAD_DIM)
    table = jax.ShapeDtypeStruct((t, HEAD_DIM), jnp.float32)
    row_blk = lambda i: (i, 0)
    const = lambda i: (0, 0)
    return pl.pallas_call(
        _norm_rope_kernel,
        out_shape=(jax.ShapeDtypeStruct((t, d), jnp.bfloat16), table, table),
        grid=(t // rows,),
        in_specs=[pl.BlockSpec((rows, d), row_blk), pl.BlockSpec((1, d), const), pl.BlockSpec((rows, 1), row_blk),
                  pl.BlockSpec((1, HEAD_DIM), const), pl.BlockSpec((1, HEAD_DIM), const)],
        out_specs=(pl.BlockSpec((rows, d), row_blk), pl.BlockSpec((rows, HEAD_DIM), row_blk),
                   pl.BlockSpec((rows, HEAD_DIM), row_blk)),
        compiler_params=_params("parallel"),
        name="rmsnorm_rope_tables",
    )(x, g.reshape(1, d), positions.reshape(t, 1), freq, sign)


def _rope(xb, cos_t, sin_t, rot):
    partner = jnp.dot(xb, rot, preferred_element_type=jnp.float32)
    return xb.astype(jnp.float32) * cos_t + partner * sin_t


def _swa_kernel(sink_ref, q_ref, kc_ref, vc_ref, kp_ref, vp_ref, cosc_ref, sinc_ref, cosp_ref, sinp_ref, rot_ref,
                o_ref):
    m_step = pl.program_id(1)
    blk = WINDOW
    cols = GQA_GROUP * blk
    rot = rot_ref[...]
    cos_b = [cosp_ref[0]] + [cosc_ref[0, t * blk:(t + 1) * blk] for t in range(SWA_BLOCKS)]
    sin_b = [sinp_ref[0]] + [sinc_ref[0, t * blk:(t + 1) * blk] for t in range(SWA_BLOCKS)]

    kj = lax.broadcasted_iota(jnp.int32, (blk, cols), 0)
    col = lax.broadcasted_iota(jnp.int32, (blk, cols), 1)
    cur_live = kj <= col % blk
    first_live = jnp.logical_or(cur_live, m_step > 0)
    col_group = lax.broadcasted_iota(jnp.int32, (1, cols), 1) // blk

    for h in range(N_KV_HEADS):
        hs = slice(h * HEAD_DIM, (h + 1) * HEAD_DIM)
        k_b = [_rope(kp_ref[0, :, hs], cos_b[0], sin_b[0], rot).astype(jnp.bfloat16)]
        v_b = [vp_ref[0, :, hs]]
        for t in range(SWA_BLOCKS):
            rows = slice(t * blk, (t + 1) * blk)
            k_b.append(_rope(kc_ref[0, rows, hs], cos_b[t + 1], sin_b[t + 1], rot).astype(jnp.bfloat16))
            v_b.append(vc_ref[0, rows, hs])
        sink = jnp.zeros((1, cols), jnp.float32)
        for g in range(GQA_GROUP):
            sink = jnp.where(col_group == g, sink_ref[h * GQA_GROUP + g], sink)
        for t in range(SWA_BLOCKS):
            rows = slice(t * blk, (t + 1) * blk)
            k = jnp.concatenate([k_b[t], k_b[t + 1]], axis=0)
            v = jnp.concatenate([v_b[t], v_b[t + 1]], axis=0)
            q_parts = []
            for g in range(GQA_GROUP):
                c0 = (h * GQA_GROUP + g) * HEAD_DIM
                q_parts.append(_rope(q_ref[0, rows, c0:c0 + HEAD_DIM], cos_b[t + 1], sin_b[t + 1], rot))
            q = jnp.concatenate(q_parts, axis=0).astype(jnp.bfloat16)
            st = lax.dot_general(k, q, (((1,), (1,)), ((), ())), preferred_element_type=jnp.float32)
            s = jnp.where(cur_live, st[blk:], st[:blk]) * (HEAD_DIM ** -0.5)
            if t == 0:
                s = jnp.where(first_live, s, NEG)
            m = jnp.maximum(jnp.max(s, axis=0, keepdims=True), sink)
            e = jnp.exp(s - m)
            l = jnp.sum(e, axis=0, keepdims=True) + jnp.exp(sink - m)
            p = e * (1.0 / l)
            pt = jnp.concatenate([jnp.where(cur_live, 0.0, p), jnp.where(cur_live, p, 0.0)], axis=0)
            o = lax.dot_general(pt.astype(jnp.bfloat16), v, (((0,), (0,)), ((), ())),
                                preferred_element_type=jnp.float32)
            for g in range(GQA_GROUP):
                c0 = (h * GQA_GROUP + g) * HEAD_DIM
                o_ref[0, rows, c0:c0 + HEAD_DIM] = o[g * blk:(g + 1) * blk].astype(o_ref.dtype)


def swa_attention(q, kvx, sink, cos_t, sin_t):
    b, s, _ = q.shape
    blk = WINDOW
    tile = SWA_BLOCKS * blk
    half = ROT_DIM // 2
    src_lane = lax.broadcasted_iota(jnp.int32, (HEAD_DIM, HEAD_DIM), 0)
    dst_lane = lax.broadcasted_iota(jnp.int32, (HEAD_DIM, HEAD_DIM), 1)
    partner_of_dst = jnp.where(dst_lane < half, dst_lane + half, dst_lane - half)
    rot = jnp.logical_and(dst_lane < ROT_DIM, src_lane == partner_of_dst).astype(jnp.bfloat16)
    cur = lambda bi, m, sk: (bi, m, 0)
    prev = lambda bi, m, sk: (bi, jnp.maximum(m * SWA_BLOCKS - 1, 0), 0)
    grid_spec = pltpu.PrefetchScalarGridSpec(
        num_scalar_prefetch=1,
        grid=(b, s // tile),
        in_specs=[
            pl.BlockSpec((1, tile, SELF_W), cur),
            pl.BlockSpec((1, tile, KV_W), cur),
            pl.BlockSpec((1, tile, KV_W), lambda bi, m, sk: (bi, m, 1)),
            pl.BlockSpec((1, blk, KV_W), prev),
            pl.BlockSpec((1, blk, KV_W), lambda bi, m, sk: (bi, jnp.maximum(m * SWA_BLOCKS - 1, 0), 1)),
            pl.BlockSpec((1, tile, HEAD_DIM), cur),
            pl.BlockSpec((1, tile, HEAD_DIM), cur),
            pl.BlockSpec((1, blk, HEAD_DIM), prev),
            pl.BlockSpec((1, blk, HEAD_DIM), prev),
            pl.BlockSpec((HEAD_DIM, HEAD_DIM), lambda bi, m, sk: (0, 0)),
        ],
        out_specs=pl.BlockSpec((1, tile, SELF_W), cur),
    )
    return pl.pallas_call(
        _swa_kernel,
        out_shape=jax.ShapeDtypeStruct((b, s, SELF_W), jnp.bfloat16),
        grid_spec=grid_spec,
        compiler_params=_params("parallel", "parallel"),
        name="swa_attention",
    )(sink, q, kvx, kvx, kvx, kvx, cos_t, sin_t, cos_t, sin_t, rot)


def kernel(x, mem, positions, norm_mix, norm_mem, norm_mlp, w_mem_kv, pool_w_in, pool_w_group, pool_scale,
           pool_w_out, attn_w_in, attn_sink, attn_w_out, mlp_w1, mlp_w2, final_norm):
    b, s, d = x.shape
    t = b * s
    bf16 = jnp.bfloat16
    h = x.reshape(t, d)
    hn, cos_t, sin_t = rmsnorm_and_rope_tables(h, norm_mix[0], positions.reshape(t))
    cos_t, sin_t = cos_t.reshape(b, s, HEAD_DIM), sin_t.reshape(b, s, HEAD_DIM)
    mem_kv_all = mem_projections(mem.reshape(b * MEM_LEN, d), norm_mem, w_mem_kv)
    hg = pss = None
    for i in range(2):
        mem_kv = mem_kv_all[i].reshape(b, MEM_LEN, 2 * XA_W)
        if i == 0:
            proj = matmul_wres([hn], pool_w_in, 0, SELF_W + XA_W).reshape(b, s, -1)
            y_self = pool_mixer(proj, pool_w_group[0].astype(bf16), pool_scale[0])
            y_mem = mem_xattn(proj, SELF_W, mem_kv)
            w_out = pool_w_out
        else:
            q = matmul_wres([hg], attn_w_in, 0, SELF_W, row_sumsq=pss).reshape(b, s, -1)
            kvx = matmul_wres([hg], attn_w_in, 0, 2 * KV_W + XA_W, bn=2 * KV_W + XA_W, col0=SELF_W, row_sumsq=pss)
            kvx = kvx.reshape(b, s, -1)
            y_self = swa_attention(q, kvx, attn_sink[0], cos_t, sin_t)
            y_mem = mem_xattn(kvx, 2 * KV_W, mem_kv)
            w_out = attn_w_out
        h, hg, pss = matmul_wres([y_self.reshape(t, SELF_W), y_mem.reshape(t, XA_W)], w_out, 0, d,
                                 epilogue="residual", res=h, out_dtype=jnp.float32, next_gain=norm_mlp[i],
                                 vmem_limit_bytes=VMEM_LIMIT_BYTES_LARGE)
        a, w2b = matmul_wres([hg], mlp_w1, i, mlp_w1.shape[2], epilogue="relu2", row_sumsq=pss, side=(mlp_w2, i))
        if i == 0:
            h, hg, pss = matmul_kacc(a, w2b, h, next_gain=norm_mix[1])
        else:
            h = matmul_kacc(a, w2b, h)
    return rmsnorm(h, final_norm, jnp.float32).reshape(b, s, d)
```

```python
import functools

import jax
import jax.numpy as jnp
from jax import lax
from jax.experimental import pallas as pl
from jax.experimental.pallas import tpu as pltpu

D_MODEL = 4096
SELF_W = 3072
XA_W = 1024
XA_HEADS = 4
XA_HEAD_DIM = 256
MEM_LEN = 256
POOL_WINDOWS = (2, 4, 8, 16)
POOL_GROUP = 768
HEAD_DIM = 128
N_Q_HEADS = 24
GQA_GROUP = 8
N_KV_HEADS = 3
KV_W = 384
WINDOW = 128
ROT_DIM = 32
ROPE_THETA = 500000.0
EPS = 1e-6
NEG = -1e30

VMEM_LIMIT_BYTES = 56 * 1024 * 1024
VMEM_LIMIT_BYTES_LARGE = 60 * 1024 * 1024
LANES = 128
MM_ROW_CHUNK = 512
SIDE_COL_BLOCK = 1024
SWA_BLOCKS = 8
POOL_HALO = 16


def _params(*sem, vmem_limit_bytes=VMEM_LIMIT_BYTES):
    return pltpu.CompilerParams(dimension_semantics=sem, vmem_limit_bytes=vmem_limit_bytes)


def _rmsnorm_kernel(x_ref, g_ref, o_ref):
    x = x_ref[...]
    ms = jnp.mean(x * x, axis=-1, keepdims=True)
    o_ref[...] = (x * lax.rsqrt(ms + EPS) * g_ref[...]).astype(o_ref.dtype)


def rmsnorm(x, g, out_dtype, rows=512):
    r, d = x.shape
    return pl.pallas_call(
        _rmsnorm_kernel,
        out_shape=jax.ShapeDtypeStruct((r, d), out_dtype),
        grid=(r // rows,),
        in_specs=[pl.BlockSpec((rows, d), lambda i: (i, 0)),
                  pl.BlockSpec((1, d), lambda i: (0, 0))],
        out_specs=pl.BlockSpec((rows, d), lambda i: (i, 0)),
        compiler_params=_params("parallel"),
        name="rmsnorm",
    )(x, g.reshape(1, d))


def _epilogue(acc, epilogue, res_ref, o_ref, rs):
    if epilogue == "relu2":
        acc = jnp.square(jnp.maximum(acc, 0.0))
    elif epilogue == "residual":
        acc = res_ref[rs, :] + acc
    o_ref[rs, :] = acc.astype(o_ref.dtype)
    return acc


def _lane_group_sumsq(h_new):
    sq = h_new * h_new
    pss = sq[:, :LANES]
    for c in range(1, sq.shape[1] // LANES):
        pss = pss + sq[:, c * LANES:(c + 1) * LANES]
    return pss


def _row_rstd(pss_ref, d):
    return lax.rsqrt(jnp.sum(pss_ref[...], axis=-1, keepdims=True) * (1.0 / d) + EPS)


def _wres_pass_and_row(step, n_warm, n_i):
    t = jnp.maximum(step - n_warm, 0)
    return t // n_i, t % n_i


def _mm_wres_kernel(*refs, n_a, k_splits, n_i, n_j, n_warm, shared_window, row_chunk, epilogue, normed,
                    emit_norm, side):
    refs = list(refs)
    a_refs = [refs.pop(0) for _ in range(n_a)]
    w_warm_ref = refs[0] if shared_window else refs.pop(0)
    w_ref = refs.pop(0)
    pss_in_ref = refs.pop(0) if normed else None
    res_ref = refs.pop(0) if epilogue == "residual" else None
    next_gain_ref = refs.pop(0) if emit_norm else None
    side_in_ref = refs.pop(0) if side else None
    o_ref = refs.pop(0)
    hb_ref = refs.pop(0) if emit_norm else None
    pss_out_ref = refs.pop(0) if emit_norm else None
    side_out_ref = refs.pop(0) if side else None
    (wbf_ref,) = refs
    step = pl.program_id(0)
    n_slots = wbf_ref.shape[0]
    warm_chunk, chunk = w_warm_ref.shape[1], w_ref.shape[1]

    @pl.when(step < n_warm)
    def _():
        row0 = pl.multiple_of(step * warm_chunk, warm_chunk)
        wbf_ref[0, pl.ds(row0, warm_chunk), :] = w_warm_ref[0].astype(jnp.bfloat16)

    @pl.when(step >= n_warm)
    def _():
        jp, i = _wres_pass_and_row(step, n_warm, n_i)

        if n_slots > 1:
            @pl.when(jp + 1 < n_j)
            def _():
                row0 = pl.multiple_of(i * chunk, chunk)
                wbf_ref[(jp + 1) % n_slots, pl.ds(row0, chunk), :] = w_ref[0].astype(jnp.bfloat16)

        use_slot = jp % n_slots
        rstd = _row_rstd(pss_in_ref, wbf_ref.shape[1]) if normed else None
        for r0 in range(0, o_ref.shape[0], row_chunk):
            rs = slice(r0, r0 + row_chunk)
            acc = None
            off = 0
            for a_ref, kw in zip(a_refs, k_splits):
                part = jnp.dot(a_ref[rs, :], wbf_ref[use_slot, pl.ds(off, kw), :], preferred_element_type=jnp.float32)
                acc = part if acc is None else acc + part
                off += kw
            if normed:
                acc = acc * rstd[rs]
            total = _epilogue(acc, epilogue, res_ref, o_ref, rs)
            if emit_norm:
                hb_ref[rs, :] = (total * next_gain_ref[...]).astype(hb_ref.dtype)
                pss_out_ref[rs, :] = _lane_group_sumsq(total)
        if side:
            for c in range(side_out_ref.shape[0]):
                cols = slice(c * side_out_ref.shape[2], (c + 1) * side_out_ref.shape[2])
                side_out_ref[c] = side_in_ref[:, cols].astype(side_out_ref.dtype)


def matmul_wres(a_list, w, layer, n_out, *, epilogue="cast", res=None, out_dtype=jnp.bfloat16, bm=1024, bn=1024,
                n_warm=8, col0=0, row_sumsq=None, next_gain=None, side=None, vmem_limit_bytes=VMEM_LIMIT_BYTES):
    emit_norm = next_gain is not None
    m = a_list[0].shape[0]
    k_total = w.shape[1]
    k_splits = tuple(a.shape[1] for a in a_list)
    n_i, n_j = m // bm, n_out // bn
    chunk, warm_chunk = k_total // n_i, k_total // n_warm
    assert sum(k_splits) == k_total and n_i * bm == m and n_j * bn == n_out
    assert chunk * n_i == k_total and warm_chunk * n_warm == k_total and chunk % 16 == 0
    assert not emit_norm or epilogue == "residual"

    def row_map(col_of_pass=None):
        def index_map(step):
            jp, i = _wres_pass_and_row(step, n_warm, n_i)
            return (i, 0 if col_of_pass is None else col_of_pass(jp))
        return index_map

    out_col = lambda jp: jp

    def w_warm_map(step):
        return (layer, pl.multiple_of(jnp.minimum(step, n_warm - 1) * warm_chunk, warm_chunk), col0)

    def w_stage_map(step):
        jp, i = _wres_pass_and_row(step, n_warm, n_i)
        staged = jnp.where(jp + 1 < n_j, i, 0)
        next_block = jnp.minimum(jp + 1, n_j - 1)
        return (layer, pl.multiple_of(staged * chunk, chunk), pl.multiple_of(col0 + next_block * bn, LANES))

    def weight_window(rows, index_map):
        return pl.BlockSpec((pl.Element(1), pl.Element(rows), pl.Element(bn)), index_map)

    def w_shared_map(step):
        warm = step < n_warm
        _, stage_row, stage_col = w_stage_map(step)
        return (layer, pl.multiple_of(jnp.where(warm, step * chunk, stage_row), chunk),
                pl.multiple_of(jnp.where(warm, col0, stage_col), LANES))

    in_specs = [pl.BlockSpec((bm, kw), row_map()) for kw in k_splits]
    if warm_chunk == chunk:
        in_specs.append(weight_window(chunk, w_shared_map))
        args = list(a_list) + [w]
    else:
        in_specs += [weight_window(warm_chunk, w_warm_map), weight_window(chunk, w_stage_map)]
        args = list(a_list) + [w, w]
    if row_sumsq is not None:
        in_specs.append(pl.BlockSpec((bm, row_sumsq.shape[1]), row_map()))
        args.append(row_sumsq)
    if epilogue == "residual":
        in_specs.append(pl.BlockSpec((bm, bn), row_map(out_col)))
        args.append(res)
    if emit_norm:
        in_specs.append(pl.BlockSpec((1, bn), lambda step: (0, _wres_pass_and_row(step, n_warm, n_i)[0])))
        args.append(next_gain.reshape(1, n_out))
    out_shape = [jax.ShapeDtypeStruct((m, n_out), out_dtype)]
    out_specs = [pl.BlockSpec((bm, bn), row_map(out_col))]
    if emit_norm:
        out_shape += [jax.ShapeDtypeStruct((m, n_out), jnp.bfloat16),
                      jax.ShapeDtypeStruct((m, n_j * LANES), jnp.float32)]
        out_specs += [pl.BlockSpec((bm, bn), row_map(out_col)), pl.BlockSpec((bm, LANES), row_map(out_col))]
    if side is not None:
        w_side, layer2 = side
        _, k2, n2 = w_side.shape
        rows2 = k2 // (n_j * n_i)
        assert rows2 * n_j * n_i == k2 and rows2 % 16 == 0
        slab = lambda step: jnp.maximum(step - n_warm, 0)
        in_specs.append(pl.BlockSpec((None, rows2, n2), lambda step: (layer2, slab(step), 0)))
        args.append(w_side)
        n2_blocks = n2 // SIDE_COL_BLOCK
        assert n2_blocks * SIDE_COL_BLOCK == n2
        out_shape.append(jax.ShapeDtypeStruct((n2_blocks, k2, SIDE_COL_BLOCK), jnp.bfloat16))
        out_specs.append(pl.BlockSpec((n2_blocks, rows2, SIDE_COL_BLOCK), lambda step: (0, slab(step), 0)))
    outs = pl.pallas_call(
        functools.partial(_mm_wres_kernel, n_a=len(a_list), k_splits=k_splits, n_i=n_i, n_j=n_j, n_warm=n_warm,
                          shared_window=warm_chunk == chunk, row_chunk=min(bm, MM_ROW_CHUNK), epilogue=epilogue,
                          normed=row_sumsq is not None, emit_norm=emit_norm, side=side is not None),
        out_shape=out_shape,
        grid=(n_warm + n_j * n_i,),
        in_specs=in_specs,
        out_specs=out_specs,
        scratch_shapes=[pltpu.VMEM((min(2, n_j), k_total, bn), jnp.bfloat16)],
        compiler_params=_params("arbitrary", vmem_limit_bytes=vmem_limit_bytes),
        name="mm_wres_" + epilogue,
    )(*args)
    return outs[0] if len(outs) == 1 else tuple(outs)


def _mm_kacc_kernel(a_ref, w_ref, res_ref, *refs, nk):
    emit_norm = len(refs) > 1
    if emit_norm:
        next_gain_ref, o_ref, hb_ref, pss_ref = refs
    else:
        (o_ref,) = refs
    k = pl.program_id(2)
    row_slices = [slice(r0, r0 + MM_ROW_CHUNK) for r0 in range(0, o_ref.shape[0], MM_ROW_CHUNK)]

    def partial_product(rs):
        return jnp.dot(a_ref[rs, :], w_ref[...], preferred_element_type=jnp.float32)

    @pl.when(k == 0)
    def _():
        for rs in row_slices:
            o_ref[rs, :] = res_ref[rs, :] + partial_product(rs)

    @pl.when(jnp.logical_and(k > 0, k < nk - 1) if emit_norm else k > 0)
    def _():
        for rs in row_slices:
            o_ref[rs, :] += partial_product(rs)

    if emit_norm:
        @pl.when(k == nk - 1)
        def _():
            for rs in row_slices:
                total = o_ref[rs, :] + partial_product(rs)
                o_ref[rs, :] = total
                hb_ref[rs, :] = (total * next_gain_ref[...]).astype(hb_ref.dtype)
                pss_ref[rs, :] = _lane_group_sumsq(total)


def matmul_kacc(a, w, res, *, next_gain=None, bm=1024, bk=4096):
    m, k_total = a.shape
    bn = w.shape[2]
    n = w.shape[0] * bn
    assert m % bm == 0 and k_total % bk == 0 and k_total // bk >= 2
    o_map = lambda j, i, k: (i, j)
    in_specs = [pl.BlockSpec((bm, bk), lambda j, i, k: (i, k)),
                pl.BlockSpec((None, bk, bn), lambda j, i, k: (j, k, 0)),
                pl.BlockSpec((bm, bn), o_map)]
    args = [a, w, res]
    out_shape = [jax.ShapeDtypeStruct((m, n), jnp.float32)]
    out_specs = [pl.BlockSpec((bm, bn), o_map)]
    if next_gain is not None:
        in_specs.append(pl.BlockSpec((1, bn), lambda j, i, k: (0, j)))
        args.append(next_gain.reshape(1, n))
        out_shape += [jax.ShapeDtypeStruct((m, n), jnp.bfloat16),
                      jax.ShapeDtypeStruct((m, (n // bn) * LANES), jnp.float32)]
        out_specs += [pl.BlockSpec((bm, bn), o_map), pl.BlockSpec((bm, LANES), o_map)]
    outs = pl.pallas_call(
        functools.partial(_mm_kacc_kernel, nk=k_total // bk),
        out_shape=out_shape,
        grid=(n // bn, m // bm, k_total // bk),
        in_specs=in_specs,
        out_specs=out_specs,
        compiler_params=_params("parallel", "parallel", "arbitrary", vmem_limit_bytes=VMEM_LIMIT_BYTES_LARGE),
        name="mm_kacc_residual",
    )(*args)
    return outs[0] if len(outs) == 1 else tuple(outs)


def _mem_proj_kernel(x_ref, g_ref, w_ref, o_ref, xn_ref):
    @pl.when(pl.program_id(1) == 0)
    def _():
        x = x_ref[...]
        ms = jnp.mean(x * x, axis=-1, keepdims=True)
        xn_ref[...] = (x * lax.rsqrt(ms + EPS) * g_ref[...]).astype(xn_ref.dtype)

    o_ref[...] = jnp.dot(xn_ref[...], w_ref[...].astype(jnp.bfloat16),
                         preferred_element_type=jnp.float32).astype(o_ref.dtype)


def mem_projections(x, g, w, *, bn=512):
    m, k_total = x.shape
    n_layers, _, n = w.shape
    assert n % bn == 0
    return pl.pallas_call(
        _mem_proj_kernel,
        out_shape=jax.ShapeDtypeStruct((n_layers, m, n), jnp.bfloat16),
        grid=(n_layers, n // bn),
        in_specs=[pl.BlockSpec((m, k_total), lambda l, j: (0, 0)),
                  pl.BlockSpec((None, 1, k_total), lambda l, j: (l, 0, 0)),
                  pl.BlockSpec((None, k_total, bn), lambda l, j: (l, 0, j))],
        out_specs=pl.BlockSpec((None, m, bn), lambda l, j: (l, 0, j)),
        scratch_shapes=[pltpu.VMEM((m, k_total), jnp.bfloat16)],
        compiler_params=_params("arbitrary", "arbitrary"),
        name="mem_proj",
    )(x, g.reshape(n_layers, 1, k_total), w)


def _pool_kernel(u_ref, halo_ref, w_ref, scale_ref, o_ref, *, ts):
    i = pl.program_id(1)
    cur = u_ref[0].astype(jnp.float32)
    halo = halo_ref[0].astype(jnp.float32)
    halo = jnp.where(i > 0, halo, 0.0)
    t1 = (i * ts + 1 + lax.broadcasted_iota(jnp.int32, (ts, 1), 0)).astype(jnp.float32)
    for g, win in enumerate(POOL_WINDOWS):
        c0, c1 = g * POOL_GROUP, (g + 1) * POOL_GROUP
        x = jnp.concatenate([halo[:, c0:c1], cur[:, c0:c1]], axis=0)
        s = x
        d = 1
        while d < win:
            s = s[d:] + s[:-d]
            d *= 2
        s = s[POOL_HALO - (win - 1):]
        mean = s / jnp.minimum(t1, float(win))
        p = (mean - cur[:, c0:c1]).astype(jnp.bfloat16)
        y = jnp.dot(p, w_ref[g], preferred_element_type=jnp.float32)
        o_ref[0, :, c0:c1] = (y * scale_ref[:, c0:c1]).astype(o_ref.dtype)


def pool_mixer(proj, w_group, scale, ts=512):
    b, s, _ = proj.shape
    hb = ts // POOL_HALO
    return pl.pallas_call(
        functools.partial(_pool_kernel, ts=ts),
        out_shape=jax.ShapeDtypeStruct((b, s, SELF_W), jnp.bfloat16),
        grid=(b, s // ts),
        in_specs=[
            pl.BlockSpec((1, ts, SELF_W), lambda bi, i: (bi, i, 0)),
            pl.BlockSpec((1, POOL_HALO, SELF_W), lambda bi, i: (bi, jnp.maximum(i * hb - 1, 0), 0)),
            pl.BlockSpec((len(POOL_WINDOWS), POOL_GROUP, POOL_GROUP), lambda bi, i: (0, 0, 0)),
            pl.BlockSpec((1, SELF_W), lambda bi, i: (0, 0)),
        ],
        out_specs=pl.BlockSpec((1, ts, SELF_W), lambda bi, i: (bi, i, 0)),
        compiler_params=_params("parallel", "parallel"),
        name="pool_mixer",
    )(proj, proj, w_group, scale.reshape(1, SELF_W))


def _xattn_kernel(*refs):
    q_refs, kv_ref, o_ref = refs[:XA_HEADS], refs[XA_HEADS], refs[XA_HEADS + 1]
    for h in range(XA_HEADS):
        hs = slice(h * XA_HEAD_DIM, (h + 1) * XA_HEAD_DIM)
        q = q_refs[h][0]
        k = kv_ref[0, :, hs]
        v = kv_ref[0, :, XA_W + h * XA_HEAD_DIM:XA_W + (h + 1) * XA_HEAD_DIM]
        s = lax.dot_general(q, k, (((1,), (1,)), ((), ())), preferred_element_type=jnp.float32)
        s = s * (XA_HEAD_DIM ** -0.5)
        m = jnp.max(s, axis=-1, keepdims=True)
        e = jnp.exp(s - m)
        l = jnp.sum(e, axis=-1, keepdims=True)
        o = jnp.dot(e.astype(jnp.bfloat16), v, preferred_element_type=jnp.float32)
        o_ref[0, :, hs] = (o / l).astype(o_ref.dtype)


def mem_xattn(proj, q_col0, mem_kv, ts=2048):
    b, s, _ = proj.shape
    qb = q_col0 // XA_HEAD_DIM
    assert qb * XA_HEAD_DIM == q_col0
    q_specs = [pl.BlockSpec((1, ts, XA_HEAD_DIM), functools.partial(lambda bi, i, h: (bi, i, qb + h), h=h))
               for h in range(XA_HEADS)]
    return pl.pallas_call(
        _xattn_kernel,
        out_shape=jax.ShapeDtypeStruct((b, s, XA_W), jnp.bfloat16),
        grid=(b, s // ts),
        in_specs=q_specs + [pl.BlockSpec((1, MEM_LEN, 2 * XA_W), lambda bi, i: (bi, 0, 0))],
        out_specs=pl.BlockSpec((1, ts, XA_W), lambda bi, i: (bi, i, 0)),
        compiler_params=_params("parallel", "parallel"),
        name="mem_xattn",
    )(*([proj] * XA_HEADS), mem_kv)


def _norm_rope_kernel(x_ref, g_ref, pos_ref, freq_ref, sign_ref, o_ref, cos_ref, sin_ref):
    _rmsnorm_kernel(x_ref, g_ref, o_ref)
    ang = pos_ref[...].astype(jnp.float32) * freq_ref[...]
    cos_ref[...] = jnp.cos(ang)
    sin_ref[...] = jnp.sin(ang) * sign_ref[...]


def rmsnorm_and_rope_tables(x, g, positions, rows=512):
    t, d = x.shape
    half = ROT_DIM // 2
    inv_freq = ROPE_THETA ** (-jnp.arange(0, ROT_DIM, 2, dtype=jnp.float32) / ROT_DIM)
    freq = jnp.concatenate([inv_freq, inv_freq, jnp.zeros((HEAD_DIM - ROT_DIM,), jnp.float32)]).reshape(1, HEAD_DIM)
    sign = jnp.concatenate([-jnp.ones((half,), jnp.float32), jnp.ones((half,), jnp.float32),
                            jnp.zeros((HEAD_DIM - ROT_DIM,), jnp.float32)]).reshape(1, HEAD_DIM)
    table = jax.ShapeDtypeStruct((t, HEAD_DIM), jnp.float32)
    row_blk = lambda i: (i, 0)
    const = lambda i: (0, 0)
    return pl.pallas_call(
        _norm_rope_kernel,
        out_shape=(jax.ShapeDtypeStruct((t, d), jnp.bfloat16), table, table),
        grid=(t // rows,),
        in_specs=[pl.BlockSpec((rows, d), row_blk), pl.BlockSpec((1, d), const), pl.BlockSpec((rows, 1), row_blk),
                  pl.BlockSpec((1, HEAD_DIM), const), pl.BlockSpec((1, HEAD_DIM), const)],
        out_specs=(pl.BlockSpec((rows, d), row_blk), pl.BlockSpec((rows, HEAD_DIM), row_blk),
                   pl.BlockSpec((rows, HEAD_DIM), row_blk)),
        compiler_params=_params("parallel"),
        name="rmsnorm_rope_tables",
    )(x, g.reshape(1, d), positions.reshape(t, 1), freq, sign)


def _rope(xb, cos_t, sin_t, rot):
    partner = jnp.dot(xb, rot, preferred_element_type=jnp.float32)
    return xb.astype(jnp.float32) * cos_t + partner * sin_t


def _swa_kernel(sink_ref, q_ref, kc_ref, vc_ref, kp_ref, vp_ref, cosc_ref, sinc_ref, cosp_ref, sinp_ref, rot_ref,
                o_ref):
    m_step = pl.program_id(1)
    blk = WINDOW
    cols = GQA_GROUP * blk
    rot = rot_ref[...]
    cos_b = [cosp_ref[0]] + [cosc_ref[0, t * blk:(t + 1) * blk] for t in range(SWA_BLOCKS)]
    sin_b = [sinp_ref[0]] + [sinc_ref[0, t * blk:(t + 1) * blk] for t in range(SWA_BLOCKS)]

    kj = lax.broadcasted_iota(jnp.int32, (blk, cols), 0)
    col = lax.broadcasted_iota(jnp.int32, (blk, cols), 1)
    cur_live = kj <= col % blk
    first_live = jnp.logical_or(cur_live, m_step > 0)
    col_group = lax.broadcasted_iota(jnp.int32, (1, cols), 1) // blk

    for h in range(N_KV_HEADS):
        hs = slice(h * HEAD_DIM, (h + 1) * HEAD_DIM)
        k_b = [_rope(kp_ref[0, :, hs], cos_b[0], sin_b[0], rot).astype(jnp.bfloat16)]
        v_b = [vp_ref[0, :, hs]]
        for t in range(SWA_BLOCKS):
            rows = slice(t * blk, (t + 1) * blk)
            k_b.append(_rope(kc_ref[0, rows, hs], cos_b[t + 1], sin_b[t + 1], rot).astype(jnp.bfloat16))
            v_b.append(vc_ref[0, rows, hs])
        sink = jnp.zeros((1, cols), jnp.float32)
        for g in range(GQA_GROUP):
            sink = jnp.where(col_group == g, sink_ref[h * GQA_GROUP + g], sink)
        for t in range(SWA_BLOCKS):
            rows = slice(t * blk, (t + 1) * blk)
            k = jnp.concatenate([k_b[t], k_b[t + 1]], axis=0)
            v = jnp.concatenate([v_b[t], v_b[t + 1]], axis=0)
            q_parts = []
            for g in range(GQA_GROUP):
                c0 = (h * GQA_GROUP + g) * HEAD_DIM
                q_parts.append(_rope(q_ref[0, rows, c0:c0 + HEAD_DIM], cos_b[t + 1], sin_b[t + 1], rot))
            q = jnp.concatenate(q_parts, axis=0).astype(jnp.bfloat16)
            st = lax.dot_general(k, q, (((1,), (1,)), ((), ())), preferred_element_type=jnp.float32)
            s = jnp.where(cur_live, st[blk:], st[:blk]) * (HEAD_DIM ** -0.5)
            if t == 0:
                s = jnp.where(first_live, s, NEG)
            m = jnp.maximum(jnp.max(s, axis=0, keepdims=True), sink)
            e = jnp.exp(s - m)
            l = jnp.sum(e, axis=0, keepdims=True) + jnp.exp(sink - m)
            p = e * (1.0 / l)
            pt = jnp.concatenate([jnp.where(cur_live, 0.0, p), jnp.where(cur_live, p, 0.0)], axis=0)
            o = lax.dot_general(pt.astype(jnp.bfloat16), v, (((0,), (0,)), ((), ())),
                                preferred_element_type=jnp.float32)
            for g in range(GQA_GROUP):
                c0 = (h * GQA_GROUP + g) * HEAD_DIM
                o_ref[0, rows, c0:c0 + HEAD_DIM] = o[g * blk:(g + 1) * blk].astype(o_ref.dtype)


def swa_attention(q, kvx, sink, cos_t, sin_t):
    b, s, _ = q.shape
    blk = WINDOW
    tile = SWA_BLOCKS * blk
    half = ROT_DIM // 2
    src_lane = lax.broadcasted_iota(jnp.int32, (HEAD_DIM, HEAD_DIM), 0)
    dst_lane = lax.broadcasted_iota(jnp.int32, (HEAD_DIM, HEAD_DIM), 1)
    partner_of_dst = jnp.where(dst_lane < half, dst_lane + half, dst_lane - half)
    rot = jnp.logical_and(dst_lane < ROT_DIM, src_lane == partner_of_dst).astype(jnp.bfloat16)
    cur = lambda bi, m, sk: (bi, m, 0)
    prev = lambda bi, m, sk: (bi, jnp.maximum(m * SWA_BLOCKS - 1, 0), 0)
    grid_spec = pltpu.PrefetchScalarGridSpec(
        num_scalar_prefetch=1,
        grid=(b, s // tile),
        in_specs=[
            pl.BlockSpec((1, tile, SELF_W), cur),
            pl.BlockSpec((1, tile, KV_W), cur),
            pl.BlockSpec((1, tile, KV_W), lambda bi, m, sk: (bi, m, 1)),
            pl.BlockSpec((1, blk, KV_W), prev),
            pl.BlockSpec((1, blk, KV_W), lambda bi, m, sk: (bi, jnp.maximum(m * SWA_BLOCKS - 1, 0), 1)),
            pl.BlockSpec((1, tile, HEAD_DIM), cur),
            pl.BlockSpec((1, tile, HEAD_DIM), cur),
            pl.BlockSpec((1, blk, HEAD_DIM), prev),
            pl.BlockSpec((1, blk, HEAD_DIM), prev),
            pl.BlockSpec((HEAD_DIM, HEAD_DIM), lambda bi, m, sk: (0, 0)),
        ],
        out_specs=pl.BlockSpec((1, tile, SELF_W), cur),
    )
    return pl.pallas_call(
        _swa_kernel,
        out_shape=jax.ShapeDtypeStruct((b, s, SELF_W), jnp.bfloat16),
        grid_spec=grid_spec,
        compiler_params=_params("parallel", "parallel"),
        name="swa_attention",
    )(sink, q, kvx, kvx, kvx, kvx, cos_t, sin_t, cos_t, sin_t, rot)


def kernel(x, mem, positions, norm_mix, norm_mem, norm_mlp, w_mem_kv, pool_w_in, pool_w_group, pool_scale,
           pool_w_out, attn_w_in, attn_sink, attn_w_out, mlp_w1, mlp_w2, final_norm):
    b, s, d = x.shape
    t = b * s
    bf16 = jnp.bfloat16
    h = x.reshape(t, d)
    hn, cos_t, sin_t = rmsnorm_and_rope_tables(h, norm_mix[0], positions.reshape(t))
    cos_t, sin_t = cos_t.reshape(b, s, HEAD_DIM), sin_t.reshape(b, s, HEAD_DIM)
    mem_kv_all = mem_projections(mem.reshape(b * MEM_LEN, d), norm_mem, w_mem_kv)
    hg = pss = None
    for i in range(2):
        mem_kv = mem_kv_all[i].reshape(b, MEM_LEN, 2 * XA_W)
        if i == 0:
            proj = matmul_wres([hn], pool_w_in, 0, SELF_W + XA_W).reshape(b, s, -1)
            y_self = pool_mixer(proj, pool_w_group[0].astype(bf16), pool_scale[0])
            y_mem = mem_xattn(proj, SELF_W, mem_kv)
            w_out = pool_w_out
        else:
            q = matmul_wres([hg], attn_w_in, 0, SELF_W, row_sumsq=pss).reshape(b, s, -1)
            kvx = matmul_wres([hg], attn_w_in, 0, 2 * KV_W + XA_W, bn=2 * KV_W + XA_W, col0=SELF_W, row_sumsq=pss)
            kvx = kvx.reshape(b, s, -1)
            y_self = swa_attention(q, kvx, attn_sink[0], cos_t, sin_t)
            y_mem = mem_xattn(kvx, 2 * KV_W, mem_kv)
            w_out = attn_w_out
        h, hg, pss = matmul_wres([y_self.reshape(t, SELF_W), y_mem.reshape(t, XA_W)], w_out, 0, d,
                                 epilogue="residual", res=h, out_dtype=jnp.float32, next_gain=norm_mlp[i], n_warm=16,
                                 vmem_limit_bytes=VMEM_LIMIT_BYTES_LARGE)
        a, w2b = matmul_wres([hg], mlp_w1, i, mlp_w1.shape[2], epilogue="relu2", row_sumsq=pss, side=(mlp_w2, i))
        if i == 0:
            h, hg, pss = matmul_kacc(a, w2b, h, next_gain=norm_mix[1])
        else:
            h = matmul_kacc(a, w2b, h)
    return rmsnorm(h, final_norm, jnp.float32).reshape(b, s, d)
```
